```python
import jax, jax.numpy as jnp
from jax import lax
import numpy as np

D_MODEL = 1024
BATCH = 2
SEQ = 16384
DEPTH = 2
DEC_BATCH = 8
DEC_SEQ = 4096
PAST_LEN = 128

FNET_GROUPS = 4
FNET_GROUP_DIM = 64
FNET_WIDTH = FNET_GROUPS * FNET_GROUP_DIM
GLA_HEADS = 4
GLA_DK = 96
GLA_DV = 192
GLA_QK_WIDTH = GLA_HEADS * GLA_DK
GLA_V_WIDTH = GLA_HEADS * GLA_DV
GLA_GATE_RANK = 16
GLA_TAU = 16.0
GLA_CHUNK = 64
CONV_WIDTH = 512
MLSTM_HEADS = 4
MLSTM_DH = 128
MLSTM_WIDTH = MLSTM_HEADS * MLSTM_DH
MLSTM_CHUNK = 64
D_FF = 2816
PLE_DIM = 256
N_EVEN = (DEPTH + 1) // 2
N_ODD = DEPTH // 2
EVEN_IN = FNET_WIDTH + 2 * GLA_QK_WIDTH + 2 * GLA_V_WIDTH + 2 * GLA_GATE_RANK
EVEN_MIX = FNET_WIDTH + GLA_V_WIDTH
ODD_IN = 3 * CONV_WIDTH + 4 * MLSTM_WIDTH + 4 * MLSTM_HEADS
ODD_MIX = CONV_WIDTH + MLSTM_WIDTH
EPS = 1e-6

kernel_name = 'hybrid_bidir_fnet_gla_shortconv_mlstm'


def rmsnorm(x, g):
    xf = x.astype(jnp.float32)
    y = xf * lax.rsqrt(jnp.mean(xf * xf, axis=-1, keepdims=True) + EPS)
    return (y * g.astype(jnp.float32)).astype(x.dtype)


def dwconv3(x, w, b):
    y = lax.conv_general_dilated(x, w[:, None, :].astype(x.dtype), window_strides=(1,),
                                 padding=((1, 1),), dimension_numbers=('NWC', 'WIO', 'NWC'),
                                 feature_group_count=x.shape[-1])
    return y + b.astype(x.dtype)


def to_heads(t, h):
    b, s, _ = t.shape
    return t.reshape(b, s, h, -1).transpose(0, 2, 1, 3)


def head_rmsnorm(o, g, dtype):
    o = o * lax.rsqrt(jnp.mean(o * o, axis=-1, keepdims=True) + EPS)
    b, h, s, d = o.shape
    o = o.transpose(0, 2, 1, 3).reshape(b, s, h * d)
    return (o * g.astype(jnp.float32)).astype(dtype)


def flip_seq(t):
    return jnp.flip(t, axis=2)


def gla_causal(q, k, v, log_a):
    b, h, s, dk = q.shape
    dv = v.shape[-1]
    n = s // GLA_CHUNK
    q = q.reshape(b, h, n, GLA_CHUNK, dk)
    k = k.reshape(b, h, n, GLA_CHUNK, dk)
    log_a = log_a.reshape(b, h, n, GLA_CHUNK, dk)
    v = v.reshape(b, h, n, GLA_CHUNK, dv)
    cum = jnp.cumsum(log_a, axis=3)
    tot = cum[:, :, :, -1:, :]
    q_in = q * jnp.exp(cum)
    k_in = k * jnp.exp(-cum)
    k_out = k * jnp.exp(tot - cum)
    mask = jnp.tril(jnp.ones((GLA_CHUNK, GLA_CHUNK), dtype=bool))
    att = jnp.where(mask, jnp.einsum('bhnld,bhnmd->bhnlm', q_in, k_in), 0.0)
    o_intra = jnp.einsum('bhnlm,bhnme->bhnle', att, v)
    contrib = jnp.einsum('bhnld,bhnle->bhnde', k_out, v)
    decay = jnp.exp(tot[:, :, :, 0, :])

    def step(state, xs):
        dec, con = xs
        return dec[..., None] * state + con, state

    _, prev = lax.scan(step, jnp.zeros((b, h, dk, dv), jnp.float32),
                       (jnp.moveaxis(decay, 2, 0), jnp.moveaxis(contrib, 2, 0)))
    prev = jnp.moveaxis(prev, 0, 2)
    o_inter = jnp.einsum('bhnld,bhnde->bhnle', q_in, prev)
    return (o_intra + o_inter).reshape(b, h, s, dv)


def mlstm_causal(q, k, v, i_pre, f_pre):
    b, h, s, dk = q.shape
    dv = v.shape[-1]
    n = s // MLSTM_CHUNK
    L = MLSTM_CHUNK
    q = q.reshape(b, h, n, L, dk)
    k = k.reshape(b, h, n, L, dk)
    v = v.reshape(b, h, n, L, dv)
    ig = i_pre.reshape(b, h, n, L)
    cum = jnp.cumsum(jax.nn.log_sigmoid(f_pre.reshape(b, h, n, L)), axis=-1)
    tot = cum[..., -1]
    w_end = tot[..., None] - cum + ig
    m_loc = jnp.max(w_end, axis=-1)
    e = jnp.exp(w_end - m_loc[..., None])
    c_loc = jnp.einsum('bhnl,bhnld,bhnle->bhnde', e, k, v)
    n_loc = jnp.einsum('bhnl,bhnld->bhnd', e, k)

    def step(carry, xs):
        c, nv, m = carry
        g, ml, cl, nl = xs
        m_new = jnp.maximum(g + m, ml)
        a = jnp.exp(g + m - m_new)
        bb = jnp.exp(ml - m_new)
        c_new = a[..., None, None] * c + bb[..., None, None] * cl
        n_new = a[..., None] * nv + bb[..., None] * nl
        return (c_new, n_new, m_new), (c, nv, m)

    init = (jnp.zeros((b, h, dk, dv), jnp.float32), jnp.zeros((b, h, dk), jnp.float32),
            jnp.zeros((b, h), jnp.float32))
    xs = (jnp.moveaxis(tot, 2, 0), jnp.moveaxis(m_loc, 2, 0),
          jnp.moveaxis(c_loc, 2, 0), jnp.moveaxis(n_loc, 2, 0))
    _, (c_prev, n_prev, m_prev) = lax.scan(step, init, xs)
    c_prev = jnp.moveaxis(c_prev, 0, 2)
    n_prev = jnp.moveaxis(n_prev, 0, 2)
    m_prev = jnp.moveaxis(m_prev, 0, 2)
    mask = jnp.tril(jnp.ones((L, L), dtype=bool))
    d = jnp.where(mask, cum[..., :, None] - cum[..., None, :] + ig[..., None, :], -jnp.inf)
    lw = cum + m_prev[..., None]
    m_t = jnp.maximum(lw, jnp.max(d, axis=-1))
    p = jnp.exp(d - m_t[..., None])
    a_t = jnp.exp(lw - m_t)
    s_qk = jnp.einsum('bhnld,bhnmd->bhnlm', q, k) * p
    num = (jnp.einsum('bhnlm,bhnme->bhnle', s_qk, v)
           + a_t[..., None] * jnp.einsum('bhnld,bhnde->bhnle', q, c_prev))
    den = jnp.sum(s_qk, axis=-1) + a_t * jnp.einsum('bhnld,bhnd->bhnl', q, n_prev)
    hout = num / jnp.maximum(jnp.abs(den), jnp.exp(-m_t))[..., None]
    return hout.reshape(b, h, s, dv)


def even_mixer(xn, w_in, gla_w2_f, gla_b_f, gla_w2_b, gla_b_b, gla_norm, w_out):
    proj = xn @ w_in
    cuts = np.cumsum([FNET_WIDTH, GLA_QK_WIDTH, GLA_QK_WIDTH, GLA_V_WIDTH, GLA_V_WIDTH]).tolist()
    u, q, k, v, r, g = jnp.split(proj, cuts, axis=-1)
    g_f = g[..., :GLA_GATE_RANK]
    g_b = g[..., GLA_GATE_RANK:]
    b, s, _ = u.shape
    uf = u.astype(jnp.float32).reshape(b, s, FNET_GROUPS, FNET_GROUP_DIM)
    y_f = jnp.real(jnp.fft.fft2(uf, axes=(1, 3), norm='ortho')).reshape(b, s, FNET_WIDTH).astype(xn.dtype)
    qh = to_heads(q, GLA_HEADS).astype(jnp.float32) * (GLA_DK ** -0.5)
    kh = to_heads(k, GLA_HEADS).astype(jnp.float32)
    vh = to_heads(v, GLA_HEADS).astype(jnp.float32)
    la_f = to_heads(jax.nn.log_sigmoid((g_f @ gla_w2_f + gla_b_f).astype(jnp.float32)) / GLA_TAU, GLA_HEADS)
    la_b = to_heads(jax.nn.log_sigmoid((g_b @ gla_w2_b + gla_b_b).astype(jnp.float32)) / GLA_TAU, GLA_HEADS)
    o = (gla_causal(qh, kh, vh, la_f)
         + flip_seq(gla_causal(flip_seq(qh), flip_seq(kh), flip_seq(vh), flip_seq(la_b))))
    y_g = head_rmsnorm(o, gla_norm, xn.dtype) * jax.nn.silu(r)
    return jnp.concatenate([y_f, y_g], axis=-1) @ w_out


def odd_mixer(xn, w_in, conv_w, conv_b, gate_bias, mlstm_norm, w_out):
    proj = xn @ w_in
    cuts = np.cumsum([CONV_WIDTH] * 3 + [MLSTM_WIDTH] * 4).tolist()
    sb, sc, sh, q, k, v, og, gates = jnp.split(proj, cuts, axis=-1)
    y_c = sb * dwconv3(sc * sh, conv_w, conv_b)
    qh = to_heads(q, MLSTM_HEADS).astype(jnp.float32) * (MLSTM_DH ** -0.5)
    kh = to_heads(k, MLSTM_HEADS).astype(jnp.float32)
    vh = to_heads(v, MLSTM_HEADS).astype(jnp.float32)
    gates = (gates.astype(jnp.float32) + gate_bias.astype(jnp.float32)).transpose(0, 2, 1)
    i_f, f_f, i_b, f_b = jnp.split(gates, 4, axis=1)
    hm = (mlstm_causal(qh, kh, vh, i_f, f_f)
          + flip_seq(mlstm_causal(flip_seq(qh), flip_seq(kh), flip_seq(vh), flip_seq(i_b), flip_seq(f_b))))
    y_m = head_rmsnorm(hm, mlstm_norm, xn.dtype) * jax.nn.sigmoid(og)
    return jnp.concatenate([y_c, y_m], axis=-1) @ w_out


def conv_ffn(xn, w_up, conv_w, conv_b, w_down):
    gate, val = jnp.split(xn @ w_up, 2, axis=-1)
    return (jax.nn.silu(dwconv3(gate, conv_w, conv_b)) * val) @ w_down


def trunk(x, p, e_norm, e_w_in, e_gla_w2_f, e_gla_b_f, e_gla_w2_b, e_gla_b_b, e_gla_norm, e_w_out,
          o_norm, o_w_in, o_conv_w, o_conv_b, o_gate_bias, o_mlstm_norm, o_w_out,
          ffn_norm, ffn_w_up, ffn_conv_w, ffn_conv_b, ffn_w_down,
          ple_w, ple_gate_norm, ple_gate_w, final_norm):
    h = x
    for i in range(DEPTH):
        j = i // 2
        if i % 2 == 0:
            h = h + even_mixer(rmsnorm(h, e_norm[j]), e_w_in[j], e_gla_w2_f[j], e_gla_b_f[j],
                               e_gla_w2_b[j], e_gla_b_b[j], e_gla_norm[j], e_w_out[j])
        else:
            h = h + odd_mixer(rmsnorm(h, o_norm[j]), o_w_in[j], o_conv_w[j], o_conv_b[j],
                              o_gate_bias[j], o_mlstm_norm[j], o_w_out[j])
        h = h + conv_ffn(rmsnorm(h, ffn_norm[i]), ffn_w_up[i], ffn_conv_w[i], ffn_conv_b[i], ffn_w_down[i])
        gate = jax.nn.sigmoid((rmsnorm(h, ple_gate_norm[i]) @ ple_gate_w[i]).astype(jnp.float32)).astype(h.dtype)
        h = h + (p[i] @ ple_w[i]) * gate
    return rmsnorm(h, final_norm)


def setup_inputs(seed: int = 0) -> dict:
    key = jax.random.key(seed)
    ks = iter(jax.random.split(key, 40))

    def nrm(shape, scale):
        return jax.random.normal(next(ks), shape, jnp.float32) * scale

    def gain(shape):
        return 1.0 + nrm(shape, 0.05)

    fbias = jnp.linspace(3.0, 6.0, MLSTM_HEADS, dtype=jnp.float32)
    return {
        'x_prompt': nrm((BATCH, SEQ, D_MODEL), 1.0),
        'x_sample': nrm((DEC_BATCH, DEC_SEQ, D_MODEL), 1.0),
        'p_prompt': nrm((DEPTH, BATCH, SEQ, PLE_DIM), 1.0),
        'p_sample': nrm((DEPTH, DEC_BATCH, DEC_SEQ, PLE_DIM), 1.0),
        'e_norm': gain((N_EVEN, D_MODEL)),
        'e_w_in': nrm((N_EVEN, D_MODEL, EVEN_IN), D_MODEL ** -0.5),
        'e_gla_w2_f': nrm((N_EVEN, GLA_GATE_RANK, GLA_QK_WIDTH), GLA_GATE_RANK ** -0.5),
        'e_gla_b_f': nrm((N_EVEN, GLA_QK_WIDTH), 0.1),
        'e_gla_w2_b': nrm((N_EVEN, GLA_GATE_RANK, GLA_QK_WIDTH), GLA_GATE_RANK ** -0.5),
        'e_gla_b_b': nrm((N_EVEN, GLA_QK_WIDTH), 0.1),
        'e_gla_norm': gain((N_EVEN, GLA_V_WIDTH)),
        'e_w_out': nrm((N_EVEN, EVEN_MIX, D_MODEL), EVEN_MIX ** -0.5),
        'o_norm': gain((N_ODD, D_MODEL)),
        'o_w_in': nrm((N_ODD, D_MODEL, ODD_IN), D_MODEL ** -0.5),
        'o_conv_w': nrm((N_ODD, 3, CONV_WIDTH), 3 ** -0.5),
        'o_conv_b': nrm((N_ODD, CONV_WIDTH), 0.02),
        'o_gate_bias': jnp.concatenate([nrm((N_ODD, MLSTM_HEADS), 0.1),
                                        fbias + nrm((N_ODD, MLSTM_HEADS), 0.1),
                                        nrm((N_ODD, MLSTM_HEADS), 0.1),
                                        fbias + nrm((N_ODD, MLSTM_HEADS), 0.1)], axis=-1),
        'o_mlstm_norm': gain((N_ODD, MLSTM_WIDTH)),
        'o_w_out': nrm((N_ODD, ODD_MIX, D_MODEL), ODD_MIX ** -0.5),
        'ffn_norm': gain((DEPTH, D_MODEL)),
        'ffn_w_up': nrm((DEPTH, D_MODEL, 2 * D_FF), D_MODEL ** -0.5),
        'ffn_conv_w': nrm((DEPTH, 3, D_FF), 3 ** -0.5),
        'ffn_conv_b': nrm((DEPTH, D_FF), 0.02),
        'ffn_w_down': nrm((DEPTH, D_FF, D_MODEL), D_FF ** -0.5),
        'ple_w': nrm((DEPTH, PLE_DIM, D_MODEL), PLE_DIM ** -0.5),
        'ple_gate_norm': gain((DEPTH, D_MODEL)),
        'ple_gate_w': nrm((DEPTH, D_MODEL, D_MODEL), D_MODEL ** -0.5),
        'final_norm': gain((D_MODEL,)),
    }


def reference(x_prompt, x_sample, p_prompt, p_sample, e_norm, e_w_in, e_gla_w2_f, e_gla_b_f,
              e_gla_w2_b, e_gla_b_b, e_gla_norm, e_w_out, o_norm, o_w_in, o_conv_w, o_conv_b,
              o_gate_bias, o_mlstm_norm, o_w_out, ffn_norm, ffn_w_up, ffn_conv_w, ffn_conv_b,
              ffn_w_down, ple_w, ple_gate_norm, ple_gate_w, final_norm):
    y_prompt = trunk(x_prompt, p_prompt, e_norm, e_w_in, e_gla_w2_f, e_gla_b_f, e_gla_w2_b, e_gla_b_b,
                     e_gla_norm, e_w_out, o_norm, o_w_in, o_conv_w, o_conv_b, o_gate_bias, o_mlstm_norm,
                     o_w_out, ffn_norm, ffn_w_up, ffn_conv_w, ffn_conv_b, ffn_w_down,
                     ple_w, ple_gate_norm, ple_gate_w, final_norm)
    y_sample = trunk(x_sample, p_sample, e_norm, e_w_in, e_gla_w2_f, e_gla_b_f, e_gla_w2_b, e_gla_b_b,
                     e_gla_norm, e_w_out, o_norm, o_w_in, o_conv_w, o_conv_b, o_gate_bias, o_mlstm_norm,
                     o_w_out, ffn_norm, ffn_w_up, ffn_conv_w, ffn_conv_b, ffn_w_down,
                     ple_w, ple_gate_norm, ple_gate_w, final_norm)
    return (y_prompt, y_sample)
```

```python
import functools
import math

import numpy as np
import jax
import jax.numpy as jnp
from jax import lax
from jax.experimental import pallas as pl
from jax.experimental.pallas import tpu as pltpu

F32 = jnp.float32
BF16 = jnp.bfloat16
EPS = 1e-6

FNET_GROUPS = 4
FNET_GROUP_DIM = 64
FNET_WIDTH = FNET_GROUPS * FNET_GROUP_DIM
GLA_HEADS = 4
GLA_DK = 96
GLA_DV = 192
GLA_GATE_RANK = 16
GLA_TAU = 16.0
CONV_WIDTH = 512
MLSTM_HEADS = 4
MLSTM_DH = 128
MLSTM_WIDTH = MLSTM_HEADS * MLSTM_DH
CHUNK = 64

GLA_DK_PAD = 128
GLA_DV_PAD = 256
GLA_QK_PAD = GLA_HEADS * GLA_DK_PAD
GLA_V_PAD = GLA_HEADS * GLA_DV_PAD
GATE_PAD = 128

V7X_VMEM_BYTES = 64 * 1024 * 1024
VMEM_LIMIT = V7X_VMEM_BYTES - 8 * 1024 * 1024

ROW_TILE = 512
HALO = 8
HALO_BF16 = 16


def _cparams(*sem):
    return pltpu.CompilerParams(dimension_semantics=sem, vmem_limit_bytes=VMEM_LIMIT)


def _rms(x, g):
    ms = jnp.mean(x * x, axis=-1, keepdims=True)
    return x * lax.rsqrt(ms + EPS) * g


def _split_bf16(x, n):
    parts = []
    r = x
    for _ in range(n):
        p = r.astype(BF16)
        parts.append(p)
        r = r - p.astype(F32)
    return parts


def _dot(a, b):
    return jnp.dot(a, b, preferred_element_type=F32)


def _dot_nt(a, b):
    return lax.dot_general(a, b, (((1,), (1,)), ((), ())), preferred_element_type=F32)


def _dot_tn(a, b):
    return lax.dot_general(a, b, (((0,), (0,)), ((), ())), preferred_element_type=F32)


def _tri_masks(n, reverse):
    r = lax.broadcasted_iota(jnp.int32, (n, n), 0)
    c = lax.broadcasted_iota(jnp.int32, (n, n), 1)
    mask = (c >= r) if reverse else (c <= r)
    return mask, mask.astype(BF16), ((r >= c) if reverse else (r <= c)).astype(BF16)


def _cumsum_rows(tri_bf, x):
    acc = None
    for p in _split_bf16(x, 3):
        t = _dot(tri_bf, p)
        acc = t if acc is None else acc + t
    return acc


def _cumsum_cols(x, tri_t_bf):
    acc = None
    for p in _split_bf16(x, 3):
        t = _dot(p, tri_t_bf)
        acc = t if acc is None else acc + t
    return acc


EVEN_SEGS = (FNET_WIDTH, GLA_QK_PAD, GLA_QK_PAD, GLA_V_PAD, GLA_V_PAD, GATE_PAD)
EVEN_W = sum(EVEN_SEGS)


def _inproj_even_kernel(h_ref, g_ref, w_ref, dft_ref, a_ref, b_ref, q_ref, k_ref, v_ref, r_ref, gg_ref):
    xn = _rms(h_ref[...], g_ref[...]).astype(BF16)
    offs = np.cumsum((0,) + EVEN_SEGS)

    def seg(i):
        return _dot(xn, w_ref[:, offs[i]:offs[i + 1]])

    u = seg(0).astype(BF16)
    ab = _dot(u, dft_ref[...])
    a_ref[...] = ab[:, :FNET_WIDTH].astype(BF16)
    b_ref[...] = ab[:, FNET_WIDTH:].astype(BF16)
    q_ref[...] = seg(1).astype(BF16)
    k_ref[...] = seg(2).astype(BF16)
    v_ref[...] = seg(3).astype(BF16)
    r_ref[...] = seg(4).astype(BF16)
    gg_ref[...] = seg(5)[:, :2 * GLA_GATE_RANK]


def _inproj_even(h2d, gain, w_pad, dft_cs, tm):
    t, d = h2d.shape
    row = lambda i: (i, 0)
    const = lambda i: (0, 0)
    widths = (FNET_WIDTH, FNET_WIDTH, GLA_QK_PAD, GLA_QK_PAD, GLA_V_PAD, GLA_V_PAD, 2 * GLA_GATE_RANK)
    dts = (BF16,) * 6 + (F32,)
    return pl.pallas_call(
        _inproj_even_kernel,
        grid=(t // tm,),
        in_specs=[pl.BlockSpec((tm, d), row), pl.BlockSpec((1, d), const),
                  pl.BlockSpec((d, EVEN_W), const), pl.BlockSpec((FNET_WIDTH, 2 * FNET_WIDTH), const)],
        out_specs=[pl.BlockSpec((tm, w), row) for w in widths],
        out_shape=[jax.ShapeDtypeStruct((t, w), dt) for w, dt in zip(widths, dts)],
        compiler_params=_cparams("parallel"),
        name="inproj_even",
    )(h2d, gain, w_pad, dft_cs)


ODD_GATES = 4 * MLSTM_HEADS
ODD_W = 3 * CONV_WIDTH + 4 * MLSTM_WIDTH + GATE_PAD


def _inproj_odd_kernel(h_ref, g_ref, w_ref, gb_ref, sb_ref, t_ref, q_ref, k_ref, v_ref, og_ref, gt_ref):
    xn = _rms(h_ref[...], g_ref[...]).astype(BF16)
    cw = CONV_WIDTH

    def seg(lo, n):
        return _dot(xn, w_ref[:, lo:lo + n])

    sb_ref[...] = seg(0, cw).astype(BF16)
    t_ref[...] = (seg(cw, cw) * seg(2 * cw, cw)).astype(BF16)
    base = 3 * cw
    q_ref[...] = seg(base, MLSTM_WIDTH).astype(BF16)
    k_ref[...] = seg(base + MLSTM_WIDTH, MLSTM_WIDTH).astype(BF16)
    v_ref[...] = seg(base + 2 * MLSTM_WIDTH, MLSTM_WIDTH).astype(BF16)
    og_ref[...] = seg(base + 3 * MLSTM_WIDTH, MLSTM_WIDTH).astype(BF16)
    gt_ref[...] = seg(base + 4 * MLSTM_WIDTH, GATE_PAD)[:, :ODD_GATES] + gb_ref[...]


def _inproj_odd(h2d, gain, w_pad, gate_bias, tm):
    t, d = h2d.shape
    row = lambda i: (i, 0)
    const = lambda i: (0, 0)
    widths = (CONV_WIDTH, CONV_WIDTH) + (MLSTM_WIDTH,) * 4 + (ODD_GATES,)
    dts = (BF16,) * 6 + (F32,)
    return pl.pallas_call(
        _inproj_odd_kernel,
        grid=(t // tm,),
        in_specs=[pl.BlockSpec((tm, d), row), pl.BlockSpec((1, d), const),
                  pl.BlockSpec((d, ODD_W), const), pl.BlockSpec((1, ODD_GATES), const)],
        out_specs=[pl.BlockSpec((tm, w), row) for w in widths],
        out_shape=[jax.ShapeDtypeStruct((t, w), dt) for w, dt in zip(widths, dts)],
        compiler_params=_cparams("parallel"),
        name="inproj_odd",
    )(h2d, gain, w_pad, gate_bias)


def _fnet_factors(s):
    lg = int(round(math.log2(s)))
    assert 2 ** lg == s
    n1 = 2 ** ((lg + 1) // 2)
    return n1, s // n1


@functools.lru_cache(maxsize=None)
def _fnet_tables(s):
    n1, n2 = _fnet_factors(s)
    k1 = np.arange(n1, dtype=np.int64)[None, :, None]
    m1 = np.arange(n1, dtype=np.int64)[None, None, :]
    j2 = np.arange(n2, dtype=np.int64)[:, None, None]
    ang = 2.0 * np.pi * ((k1 * (j2 + n2 * m1)) % s).astype(np.float64) / s
    gc = np.cos(ang) / np.sqrt(n1)
    gs = np.sin(ang) / np.sqrt(n1)
    g = np.concatenate([np.concatenate([gc, -gs], axis=2), np.concatenate([-gs, -gc], axis=2)], axis=1)
    k2 = np.arange(n2, dtype=np.int64)[:, None]
    m2 = np.arange(n2, dtype=np.int64)[None, :]
    ang2 = 2.0 * np.pi * ((k2 * m2) % n2).astype(np.float64) / n2
    f2 = np.concatenate([np.cos(ang2), np.sin(ang2)], axis=1) / np.sqrt(n2)
    return (jnp.asarray(g.astype(np.float32)).astype(BF16),
            jnp.asarray(f2.astype(np.float32)).astype(BF16))


@functools.lru_cache(maxsize=None)
def _channel_dft():
    j = np.arange(FNET_WIDTH)
    same = (j[:, None] // FNET_GROUP_DIM) == (j[None, :] // FNET_GROUP_DIM)
    ang = 2.0 * np.pi * (((j[:, None] % FNET_GROUP_DIM) * (j[None, :] % FNET_GROUP_DIM)) % FNET_GROUP_DIM) / FNET_GROUP_DIM
    c = np.where(same, np.cos(ang), 0.0) / np.sqrt(FNET_GROUP_DIM)
    sn = np.where(same, np.sin(ang), 0.0) / np.sqrt(FNET_GROUP_DIM)
    return jnp.asarray(np.concatenate([c, sn], axis=1).astype(np.float32)).astype(BF16)


def _fnet1_kernel(a_ref, b_ref, g_ref, p_ref, q_ref, *, n1, tn2, c):
    for t in range(tn2):
        sl = slice(t * c, (t + 1) * c)
        rhs = jnp.concatenate([a_ref[:, sl], b_ref[:, sl]], axis=0)
        res = _dot(g_ref[t], rhs)
        p_ref[:, sl] = res[:n1].astype(BF16)
        q_ref[:, sl] = res[n1:].astype(BF16)


def _fnet2_kernel(p_ref, q_ref, f_ref, y_ref, *, n2, tk, c):
    for t in range(tk):
        rs = slice(t * n2, (t + 1) * n2)
        rhs = jnp.concatenate([p_ref[rs, :], q_ref[rs, :]], axis=0)
        y_ref[:, t * c:(t + 1) * c] = _dot(f_ref[...], rhs).astype(BF16)


def _fnet(a, b):
    bsz, s, c = a.shape
    n1, n2 = _fnet_factors(s)
    g_tab, f2_tab = _fnet_tables(s)
    tn2 = min(8, n2)
    tk = min(8, n1)
    av = a.reshape(bsz, n1, n2 * c)
    bv = b.reshape(bsz, n1, n2 * c)
    blk = pl.BlockSpec((None, n1, tn2 * c), lambda j, i: (i, 0, j))
    p, q = pl.pallas_call(
        functools.partial(_fnet1_kernel, n1=n1, tn2=tn2, c=c),
        grid=(n2 // tn2, bsz),
        in_specs=[blk, blk, pl.BlockSpec((tn2, 2 * n1, 2 * n1), lambda j, i: (j, 0, 0))],
        out_specs=[blk, blk],
        out_shape=[jax.ShapeDtypeStruct((bsz, n1, n2 * c), BF16)] * 2,
        compiler_params=_cparams("parallel", "parallel"),
        name="fnet_stage1",
    )(av, bv, g_tab)
    pv = p.reshape(bsz, n1 * n2, c)
    qv = q.reshape(bsz, n1 * n2, c)
    rblk = pl.BlockSpec((None, tk * n2, c), lambda i, j: (i, j, 0))
    y = pl.pallas_call(
        functools.partial(_fnet2_kernel, n2=n2, tk=tk, c=c),
        grid=(bsz, n1 // tk),
        in_specs=[rblk, rblk, pl.BlockSpec((n2, 2 * n2), lambda i, j: (0, 0))],
        out_specs=pl.BlockSpec((None, n2, tk * c), lambda i, j: (i, 0, j)),
        out_shape=jax.ShapeDtypeStruct((bsz, n2, n1 * c), BF16),
        compiler_params=_cparams("parallel", "parallel"),
        name="fnet_stage2",
    )(pv, qv, f2_tab)
    return y.reshape(bsz, s, c)


def _gla_chunk(q, k, v, g, w2_parts, bias, masks, state_ref, reverse):
    mask, tri_bf, _ = masks
    n = q.shape[0]
    g_parts = _split_bf16(g, 2)
    pre = (_dot(g_parts[0], w2_parts[0]) + _dot(g_parts[0], w2_parts[1])
           + _dot(g_parts[1], w2_parts[0])) + bias
    la = jax.nn.log_sigmoid(pre) * (1.0 / GLA_TAU)
    cum = _cumsum_rows(tri_bf, la)
    tot = cum[0:1, :] if reverse else cum[n - 1:n, :]
    qf = q.astype(F32) * (GLA_DK ** -0.5) * jnp.exp(cum)
    kf = k.astype(F32)
    k_in = kf * jnp.exp(-cum)
    k_out = kf * jnp.exp(tot - cum)
    dec = jnp.exp(tot)
    outs = []
    for h in range(GLA_HEADS):
        sl = slice(h * GLA_DK_PAD, (h + 1) * GLA_DK_PAD)
        qh = qf[:, sl].astype(BF16)
        kih = k_in[:, sl].astype(BF16)
        koh = k_out[:, sl].astype(BF16)
        vh = v[:, h * GLA_DV_PAD:(h + 1) * GLA_DV_PAD]
        att = jnp.where(mask, _dot_nt(qh, kih), 0.0).astype(BF16)
        st = state_ref[h]
        outs.append(_dot(att, vh) + _dot_nt(qh, st.astype(BF16)))
        state_ref[h] = st * dec[:, sl] + _dot_tn(vh, koh)
    return jnp.concatenate(outs, axis=1)


def _gla_scan(q_ref, k_ref, v_ref, g_ref, w2_ref, b_ref, o_ref, state_ref, reverse):
    ts = q_ref.shape[0]
    nchunk = ts // CHUNK
    masks = _tri_masks(CHUNK, reverse)
    w2_parts = _split_bf16(w2_ref[...], 2)
    bias = b_ref[...]

    @pl.when(pl.program_id(1) == 0)
    def _():
        state_ref[...] = jnp.zeros_like(state_ref)

    def body(ci, carry):
        c = (nchunk - 1 - ci) if reverse else ci
        rows = pl.ds(pl.multiple_of(c * CHUNK, CHUNK), CHUNK)
        o_ref[rows, :] = _gla_chunk(q_ref[rows, :], k_ref[rows, :], v_ref[rows, :], g_ref[rows, :],
                                    w2_parts, bias, masks, state_ref, reverse)
        return carry

    lax.fori_loop(0, nchunk, body, 0)


def _gla_fwd_kernel(q_ref, k_ref, v_ref, g_ref, w2_ref, b_ref, of_ref, state_ref, o_scr):
    _gla_scan(q_ref, k_ref, v_ref, g_ref, w2_ref, b_ref, o_scr, state_ref, False)
    of_ref[...] = o_scr[...].astype(BF16)


def _gla_bwd_kernel(q_ref, k_ref, v_ref, g_ref, w2_ref, b_ref, of_ref, r_ref, yf_ref, h_ref,
                    gn_ref, wf_ref, wg_ref, out_ref, state_ref, o_scr):
    _gla_scan(q_ref, k_ref, v_ref, g_ref, w2_ref, b_ref, o_scr, state_ref, True)
    o = o_scr[...] + of_ref[...].astype(F32)
    normed = []
    for h in range(GLA_HEADS):
        oh = o[:, h * GLA_DV_PAD:(h + 1) * GLA_DV_PAD]
        ms = jnp.sum(oh * oh, axis=-1, keepdims=True) * (1.0 / GLA_DV)
        normed.append(oh * lax.rsqrt(ms + EPS))
    yg = jnp.concatenate(normed, axis=1) * gn_ref[...]
    yg = (yg * jax.nn.silu(r_ref[...].astype(F32))).astype(BF16)
    out_ref[...] = h_ref[...] + _dot(yf_ref[...], wf_ref[...]) + _dot(yg, wg_ref[...])


def _gla(q, k, v, g, w2f, bf, w2b, bb, r, yf, h, gnorm, w_out_f, w_out_g, ts):
    bsz, s, _ = q.shape
    d = h.shape[-1]
    nblk = s // ts
    fwd = lambda b, i: (b, i, 0)
    bwd = lambda b, i: (b, nblk - 1 - i, 0)
    const = lambda b, i: (0, 0)

    def seq_specs(im):
        return [pl.BlockSpec((None, ts, GLA_QK_PAD), im), pl.BlockSpec((None, ts, GLA_QK_PAD), im),
                pl.BlockSpec((None, ts, GLA_V_PAD), im), pl.BlockSpec((None, ts, 2 * GLA_GATE_RANK), im),
                pl.BlockSpec((2 * GLA_GATE_RANK, GLA_QK_PAD), const), pl.BlockSpec((1, GLA_QK_PAD), const)]

    scratch = [pltpu.VMEM((GLA_HEADS, GLA_DV_PAD, GLA_DK_PAD), F32), pltpu.VMEM((ts, GLA_V_PAD), F32)]
    o_f = pl.pallas_call(
        _gla_fwd_kernel,
        grid=(bsz, nblk),
        in_specs=seq_specs(fwd),
        out_specs=pl.BlockSpec((None, ts, GLA_V_PAD), fwd),
        out_shape=jax.ShapeDtypeStruct((bsz, s, GLA_V_PAD), BF16),
        scratch_shapes=scratch,
        compiler_params=_cparams("parallel", "arbitrary"),
        name="gla_fwd",
    )(q, k, v, g, w2f, bf)
    return pl.pallas_call(
        _gla_bwd_kernel,
        grid=(bsz, nblk),
        in_specs=seq_specs(bwd) + [
            pl.BlockSpec((None, ts, GLA_V_PAD), bwd), pl.BlockSpec((None, ts, GLA_V_PAD), bwd),
            pl.BlockSpec((None, ts, FNET_WIDTH), bwd), pl.BlockSpec((None, ts, d), bwd),
            pl.BlockSpec((1, GLA_V_PAD), const), pl.BlockSpec((FNET_WIDTH, d), const),
            pl.BlockSpec((GLA_V_PAD, d), const)],
        out_specs=pl.BlockSpec((None, ts, d), bwd),
        out_shape=jax.ShapeDtypeStruct((bsz, s, d), F32),
        scratch_shapes=scratch,
        compiler_params=_cparams("parallel", "arbitrary"),
        name="gla_bwd_out",
    )(q, k, v, g, w2b, bb, o_f, r, yf, h, gnorm, w_out_f, w_out_g)


def _mlstm_chunk(q, k, v, gc, gr, masks, c_ref, n_ref, m_ref, reverse, gate_base):
    mask, tri_bf, tri_t_bf = masks
    n = q.shape[0]
    scale = MLSTM_DH ** -0.5
    cum_c = _cumsum_rows(tri_bf, jax.nn.log_sigmoid(gc))
    cum_r = _cumsum_cols(jax.nn.log_sigmoid(gr), tri_t_bf)
    outs = []
    for h in range(MLSTM_HEADS):
        ci = gate_base + h
        cf = gate_base + MLSTM_HEADS + h
        sl = slice(h * MLSTM_DH, (h + 1) * MLSTM_DH)
        qh, kh, vh = q[:, sl], k[:, sl], v[:, sl]
        ig_c, ig_r = gc[:, ci:ci + 1], gr[ci:ci + 1, :]
        cm_c, cm_r = cum_c[:, cf:cf + 1], cum_r[cf:cf + 1, :]
        tot = cm_c[0:1, :] if reverse else cm_c[n - 1:n, :]
        c_prev, n_prev, m_prev = c_ref[h], n_ref[h], m_ref[h]
        dmat = jnp.where(mask, cm_c - cm_r + ig_r, -jnp.inf)
        lw = cm_c + m_prev
        m_t = jnp.maximum(lw, jnp.max(dmat, axis=1, keepdims=True))
        pmat = jnp.exp(dmat - m_t)
        a_t = jnp.exp(lw - m_t)
        s_qk = _dot_nt(qh, kh) * (pmat * scale)
        num = _dot(s_qk.astype(BF16), vh) + (a_t * scale) * _dot_nt(qh, c_prev.astype(BF16))
        qn = jnp.sum(qh.astype(F32) * n_prev, axis=1, keepdims=True)
        den = jnp.sum(s_qk, axis=1, keepdims=True) + (a_t * scale) * qn
        outs.append(num / jnp.maximum(jnp.abs(den), jnp.exp(-m_t)))
        m_loc = jnp.max(tot - cm_r + ig_r, axis=1, keepdims=True)
        ke = kh.astype(F32) * jnp.exp(tot - cm_c + ig_c - m_loc)
        c_loc = _dot_tn(vh, ke.astype(BF16))
        n_loc = jnp.sum(ke, axis=0, keepdims=True)
        m_new = jnp.maximum(tot + m_prev, m_loc)
        fa = jnp.exp(tot + m_prev - m_new)
        fb = jnp.exp(m_loc - m_new)
        c_ref[h] = fa * c_prev + fb * c_loc
        n_ref[h] = fa * n_prev + fb * n_loc
        m_ref[h] = m_new
    return jnp.concatenate(outs, axis=1)


def _mlstm_scan(q_ref, k_ref, v_ref, gc_ref, gr_ref, o_ref, c_ref, n_ref, m_ref, reverse):
    ts = q_ref.shape[0]
    nchunk = ts // CHUNK
    masks = _tri_masks(CHUNK, reverse)
    gate_base = 2 * MLSTM_HEADS if reverse else 0

    @pl.when(pl.program_id(1) == 0)
    def _():
        c_ref[...] = jnp.zeros_like(c_ref)
        n_ref[...] = jnp.zeros_like(n_ref)
        m_ref[...] = jnp.zeros_like(m_ref)

    def body(ci, carry):
        c = (nchunk - 1 - ci) if reverse else ci
        rows = pl.ds(pl.multiple_of(c * CHUNK, CHUNK), CHUNK)
        o_ref[rows, :] = _mlstm_chunk(q_ref[rows, :], k_ref[rows, :], v_ref[rows, :], gc_ref[rows, :],
                                      gr_ref[c], masks, c_ref, n_ref, m_ref, reverse, gate_base)
        return carry

    lax.fori_loop(0, nchunk, body, 0)


def _mlstm_fwd_kernel(q_ref, k_ref, v_ref, gc_ref, gr_ref, hf_ref, c_ref, n_ref, m_ref, o_scr):
    _mlstm_scan(q_ref, k_ref, v_ref, gc_ref, gr_ref, o_scr, c_ref, n_ref, m_ref, False)
    hf_ref[...] = o_scr[...].astype(BF16)


def _mlstm_bwd_kernel(q_ref, k_ref, v_ref, gc_ref, gr_ref, hf_ref, og_ref, sb_ref, t_ref, tp_ref, tn_ref,
                      h_ref, mn_ref, cw_ref, cb_ref, wc_ref, wm_ref, out_ref, c_ref, n_ref, m_ref, o_scr):
    _mlstm_scan(q_ref, k_ref, v_ref, gc_ref, gr_ref, o_scr, c_ref, n_ref, m_ref, True)
    ts = q_ref.shape[0]
    o = o_scr[...] + hf_ref[...].astype(F32)
    normed = []
    for h in range(MLSTM_HEADS):
        oh = o[:, h * MLSTM_DH:(h + 1) * MLSTM_DH]
        normed.append(oh * lax.rsqrt(jnp.mean(oh * oh, axis=-1, keepdims=True) + EPS))
    ym = jnp.concatenate(normed, axis=1) * mn_ref[...]
    ym = (ym * jax.nn.sigmoid(og_ref[...].astype(F32))).astype(BF16)
    i = pl.program_id(1)
    nblk = pl.num_programs(1)
    prev = jnp.where(i < nblk - 1, tp_ref[...].astype(F32), 0.0)
    nxt = jnp.where(i > 0, tn_ref[...].astype(F32), 0.0)
    ext = jnp.concatenate([prev, t_ref[...].astype(F32), nxt], axis=0)
    rows = ts + 2 * HALO_BF16
    lo = pltpu.roll(ext, 1, axis=0)[HALO_BF16:HALO_BF16 + ts]
    hi = pltpu.roll(ext, rows - 1, axis=0)[HALO_BF16:HALO_BF16 + ts]
    cw = cw_ref[...]
    conv = lo * cw[0:1, :] + ext[HALO_BF16:HALO_BF16 + ts] * cw[1:2, :] + hi * cw[2:3, :] + cb_ref[...]
    yc = (sb_ref[...].astype(F32) * conv).astype(BF16)
    out_ref[...] = h_ref[...] + _dot(yc, wc_ref[...]) + _dot(ym, wm_ref[...])


def _mlstm(q, k, v, gates, gates_t, og, sb, t, h, mnorm, conv_w, conv_b, w_out_c, w_out_m, ts):
    bsz, s, _ = q.shape
    d = h.shape[-1]
    nblk = s // ts
    nck = ts // CHUNK
    hb = ts // HALO_BF16
    fwd = lambda b, i: (b, i, 0)
    bwd = lambda b, i: (b, nblk - 1 - i, 0)
    const = lambda b, i: (0, 0)

    def seq_specs(im, im4):
        w = MLSTM_WIDTH
        return [pl.BlockSpec((None, ts, w), im), pl.BlockSpec((None, ts, w), im), pl.BlockSpec((None, ts, w), im),
                pl.BlockSpec((None, ts, ODD_GATES), im), pl.BlockSpec((None, nck, ODD_GATES, CHUNK), im4)]

    scratch = [pltpu.VMEM((MLSTM_HEADS, MLSTM_DH, MLSTM_DH), F32), pltpu.VMEM((MLSTM_HEADS, 1, MLSTM_DH), F32),
               pltpu.VMEM((MLSTM_HEADS, 1, 1), F32), pltpu.VMEM((ts, MLSTM_WIDTH), F32)]
    h_f = pl.pallas_call(
        _mlstm_fwd_kernel,
        grid=(bsz, nblk),
        in_specs=seq_specs(fwd, lambda b, i: (b, i, 0, 0)),
        out_specs=pl.BlockSpec((None, ts, MLSTM_WIDTH), fwd),
        out_shape=jax.ShapeDtypeStruct((bsz, s, MLSTM_WIDTH), BF16),
        scratch_shapes=scratch,
        compiler_params=_cparams("parallel", "arbitrary"),
        name="mlstm_fwd",
    )(q, k, v, gates, gates_t)
    last16 = s // HALO_BF16 - 1
    halo_prev = lambda b, i: (b, jnp.maximum((nblk - 1 - i) * hb - 1, 0), 0)
    halo_next = lambda b, i: (b, jnp.minimum((nblk - i) * hb, last16), 0)
    return pl.pallas_call(
        _mlstm_bwd_kernel,
        grid=(bsz, nblk),
        in_specs=seq_specs(bwd, lambda b, i: (b, nblk - 1 - i, 0, 0)) + [
            pl.BlockSpec((None, ts, MLSTM_WIDTH), bwd), pl.BlockSpec((None, ts, MLSTM_WIDTH), bwd),
            pl.BlockSpec((None, ts, CONV_WIDTH), bwd), pl.BlockSpec((None, ts, CONV_WIDTH), bwd),
            pl.BlockSpec((None, HALO_BF16, CONV_WIDTH), halo_prev),
            pl.BlockSpec((None, HALO_BF16, CONV_WIDTH), halo_next),
            pl.BlockSpec((None, ts, d), bwd),
            pl.BlockSpec((1, MLSTM_WIDTH), const), pl.BlockSpec((3, CONV_WIDTH), const),
            pl.BlockSpec((1, CONV_WIDTH), const), pl.BlockSpec((CONV_WIDTH, d), const),
            pl.BlockSpec((MLSTM_WIDTH, d), const)],
        out_specs=pl.BlockSpec((None, ts, d), bwd),
        out_shape=jax.ShapeDtypeStruct((bsz, s, d), F32),
        scratch_shapes=scratch,
        compiler_params=_cparams("parallel", "arbitrary"),
        name="mlstm_bwd_out",
    )(q, k, v, gates, gates_t, h_f, og, sb, t, t, t, h, mnorm, conv_w, conv_b, w_out_c, w_out_m)


def _ffn_kernel(h_ref, hp_ref, hn_ref, p_ref, fg_ref, wu_ref, cw_ref, cb_ref, wd_ref, pg_ref, pgw_ref, pw_ref,
                fn_ref, out_ref, xn_scr, *, d_ff, f_tile, final):
    ts = h_ref.shape[0]
    i = pl.program_id(1)
    nblk = pl.num_programs(1)
    x = h_ref[...]
    fg = fg_ref[...]
    xn_scr[0:HALO, :] = jnp.where(i > 0, _rms(hp_ref[...], fg), 0.0).astype(BF16)
    xn_scr[HALO:HALO + ts, :] = _rms(x, fg).astype(BF16)
    xn_scr[HALO + ts:, :] = jnp.where(i < nblk - 1, _rms(hn_ref[...], fg), 0.0).astype(BF16)
    rows = ts + 2 * HALO
    xe = xn_scr[...]
    xc = xe[HALO:HALO + ts]
    acc = jnp.zeros_like(x)
    for f0 in range(0, d_ff, f_tile):
        gate = _dot(xe, wu_ref[:, f0:f0 + f_tile])
        val = _dot(xc, wu_ref[:, d_ff + f0:d_ff + f0 + f_tile])
        cw = cw_ref[:, f0:f0 + f_tile]
        lo = pltpu.roll(gate, 1, axis=0)[HALO:HALO + ts]
        hi = pltpu.roll(gate, rows - 1, axis=0)[HALO:HALO + ts]
        conv = (lo * cw[0:1, :] + gate[HALO:HALO + ts] * cw[1:2, :] + hi * cw[2:3, :]
                + cb_ref[:, f0:f0 + f_tile])
        act = (jax.nn.silu(conv) * val).astype(BF16)
        acc = acc + _dot(act, wd_ref[f0:f0 + f_tile, :])
    h2 = x + acc
    gate = jax.nn.sigmoid(_dot(_rms(h2, pg_ref[...]).astype(BF16), pgw_ref[...]))
    h3 = h2 + _dot(p_ref[...].astype(BF16), pw_ref[...]) * gate
    out_ref[...] = _rms(h3, fn_ref[...]) if final else h3


def _ffn(h, p, layer, fgain, w_up, conv_w, conv_b, w_down, pgain, pg_w, p_w, final_gain, final, ts, f_tile):
    bsz, s, d = h.shape
    d_ff = w_down.shape[0]
    nblk = s // ts
    hb = ts // HALO
    last8 = s // HALO - 1
    blk = lambda b, i: (b, i, 0)
    const = lambda b, i: (0, 0)
    once = pl.Buffered(1)

    def wspec(shape):
        return pl.BlockSpec(shape, const, pipeline_mode=once)

    return pl.pallas_call(
        functools.partial(_ffn_kernel, d_ff=d_ff, f_tile=f_tile, final=final),
        grid=(bsz, nblk),
        in_specs=[pl.BlockSpec((None, ts, d), blk),
                  pl.BlockSpec((None, HALO, d), lambda b, i: (b, jnp.maximum(i * hb - 1, 0), 0)),
                  pl.BlockSpec((None, HALO, d), lambda b, i: (b, jnp.minimum((i + 1) * hb, last8), 0)),
                  pl.BlockSpec((None, None, ts, p.shape[-1]), lambda b, i: (layer, b, i, 0)),
                  wspec((1, d)), wspec((d, 2 * d_ff)), wspec((3, d_ff)), wspec((1, d_ff)), wspec((d_ff, d)),
                  wspec((1, d)), wspec((d, d)), wspec((p.shape[-1], d)), wspec((1, d))],
        out_specs=pl.BlockSpec((None, ts, d), blk),
        out_shape=jax.ShapeDtypeStruct((bsz, s, d), F32),
        scratch_shapes=[pltpu.VMEM((ts + 2 * HALO, d), BF16)],
        compiler_params=_cparams("parallel", "parallel"),
        name="ffn_ple",
    )(h, h, h, p, fgain, w_up, conv_w, conv_b, w_down, pgain, pg_w, p_w, final_gain)


def _pad_heads_cols(w, heads, dh, dpad):
    lead = w.shape[:-1]
    w = w.reshape(lead + (heads, dh))
    w = jnp.pad(w, [(0, 0)] * len(lead) + [(0, 0), (0, dpad - dh)])
    return w.reshape(lead + (heads * dpad,))


def _pad_cols(w, width):
    return jnp.pad(w, [(0, 0)] * (w.ndim - 1) + [(0, width - w.shape[-1])])


def _even_weights(w_in, w2_f, b_f, w2_b, b_b, gnorm, w_out):
    cuts = np.cumsum([FNET_WIDTH, GLA_HEADS * GLA_DK, GLA_HEADS * GLA_DK, GLA_HEADS * GLA_DV,
                      GLA_HEADS * GLA_DV]).tolist()
    wu, wq, wk, wv, wr, wg = jnp.split(w_in, cuts, axis=-1)
    w_pad = jnp.concatenate([
        wu, _pad_heads_cols(wq, GLA_HEADS, GLA_DK, GLA_DK_PAD), _pad_heads_cols(wk, GLA_HEADS, GLA_DK, GLA_DK_PAD),
        _pad_heads_cols(wv, GLA_HEADS, GLA_DV, GLA_DV_PAD), _pad_heads_cols(wr, GLA_HEADS, GLA_DV, GLA_DV_PAD),
        _pad_cols(wg, GATE_PAD)], axis=-1).astype(BF16)
    zero = jnp.zeros((GLA_GATE_RANK, GLA_QK_PAD), F32)
    w2f = jnp.concatenate([_pad_heads_cols(w2_f, GLA_HEADS, GLA_DK, GLA_DK_PAD), zero], axis=0)
    w2b = jnp.concatenate([zero, _pad_heads_cols(w2_b, GLA_HEADS, GLA_DK, GLA_DK_PAD)], axis=0)
    bf = _pad_heads_cols(b_f[None, :], GLA_HEADS, GLA_DK, GLA_DK_PAD)
    bb = _pad_heads_cols(b_b[None, :], GLA_HEADS, GLA_DK, GLA_DK_PAD)
    gn = _pad_heads_cols(gnorm[None, :], GLA_HEADS, GLA_DV, GLA_DV_PAD)
    w_out_f = w_out[:FNET_WIDTH].astype(BF16)
    wog = w_out[FNET_WIDTH:].reshape(GLA_HEADS, GLA_DV, -1)
    wog = jnp.pad(wog, ((0, 0), (0, GLA_DV_PAD - GLA_DV), (0, 0))).reshape(GLA_V_PAD, -1).astype(BF16)
    return w_pad, w2f, bf, w2b, bb, gn, w_out_f, wog


def _trunk(x, p, e_norm, e_w_in, e_gla_w2_f, e_gla_b_f, e_gla_w2_b, e_gla_b_b, e_gla_norm, e_w_out,
           o_norm, o_w_in, o_conv_w, o_conv_b, o_gate_bias, o_mlstm_norm, o_w_out,
           ffn_norm, ffn_w_up, ffn_conv_w, ffn_conv_b, ffn_w_down,
           ple_w, ple_gate_norm, ple_gate_w, final_norm, *, ts=ROW_TILE, f_tile=256):
    bsz, s, d = x.shape
    depth = p.shape[0]
    t = bsz * s
    ts = min(ts, s)
    h = x
    for layer in range(depth):
        j = layer // 2
        h2d = h.reshape(t, d)
        if layer % 2 == 0:
            w_pad, w2f, bf, w2b, bb, gn, w_out_f, w_out_g = _even_weights(
                e_w_in[j], e_gla_w2_f[j], e_gla_b_f[j], e_gla_w2_b[j], e_gla_b_b[j], e_gla_norm[j], e_w_out[j])
            a, b, q, k, v, r, g = _inproj_even(h2d, e_norm[j][None, :], w_pad, _channel_dft(), ts)
            sh = lambda z: z.reshape(bsz, s, z.shape[-1])
            yf = _fnet(sh(a), sh(b))
            h = _gla(sh(q), sh(k), sh(v), sh(g), w2f, bf, w2b, bb, sh(r), yf, h, gn, w_out_f, w_out_g, ts)
        else:
            w_pad = _pad_cols(o_w_in[j], ODD_W).astype(BF16)
            sb, tt, q, k, v, og, gates = _inproj_odd(h2d, o_norm[j][None, :], w_pad, o_gate_bias[j][None, :], ts)
            sh = lambda z: z.reshape(bsz, s, z.shape[-1])
            gates = sh(gates)
            gates_t = gates.reshape(bsz, s // CHUNK, CHUNK, ODD_GATES).transpose(0, 1, 3, 2)
            w_out = o_w_out[j].astype(BF16)
            h = _mlstm(sh(q), sh(k), sh(v), gates, gates_t, sh(og), sh(sb), sh(tt), h,
                       o_mlstm_norm[j][None, :], o_conv_w[j], o_conv_b[j][None, :],
                       w_out[:CONV_WIDTH], w_out[CONV_WIDTH:], ts)
        h = _ffn(h, p, layer, ffn_norm[layer][None, :], ffn_w_up[layer].astype(BF16), ffn_conv_w[layer],
                 ffn_conv_b[layer][None, :], ffn_w_down[layer].astype(BF16), ple_gate_norm[layer][None, :],
                 ple_gate_w[layer].astype(BF16), ple_w[layer].astype(BF16), final_norm[None, :],
                 layer == depth - 1, ts, f_tile)
    return h


def kernel(x_prompt, x_sample, p_prompt, p_sample, e_norm, e_w_in, e_gla_w2_f, e_gla_b_f, e_gla_w2_b, e_gla_b_b, e_gla_norm, e_w_out, o_norm, o_w_in, o_conv_w, o_conv_b, o_gate_bias, o_mlstm_norm, o_w_out, ffn_norm, ffn_w_up, ffn_conv_w, ffn_conv_b, ffn_w_down, ple_w, ple_gate_norm, ple_gate_w, final_norm):
    weights = (e_norm, e_w_in, e_gla_w2_f, e_gla_b_f, e_gla_w2_b, e_gla_b_b, e_gla_norm, e_w_out,
               o_norm, o_w_in, o_conv_w, o_conv_b, o_gate_bias, o_mlstm_norm, o_w_out,
               ffn_norm, ffn_w_up, ffn_conv_w, ffn_conv_b, ffn_w_down,
               ple_w, ple_gate_norm, ple_gate_w, final_norm)
    return (_trunk(x_prompt, p_prompt, *weights), _trunk(x_sample, p_sample, *weights))
```

```python
import functools
import math

import numpy as np
import jax
import jax.numpy as jnp
from jax import lax
from jax.experimental import pallas as pl
from jax.experimental.pallas import tpu as pltpu

F32 = jnp.float32
BF16 = jnp.bfloat16
EPS = 1e-6

FNET_GROUPS = 4
FNET_GROUP_DIM = 64
FNET_WIDTH = FNET_GROUPS * FNET_GROUP_DIM
GLA_HEADS = 4
GLA_DK = 96
GLA_DV = 192
GLA_GATE_RANK = 16
GLA_TAU = 16.0
CONV_WIDTH = 512
MLSTM_HEADS = 4
MLSTM_DH = 128
MLSTM_WIDTH = MLSTM_HEADS * MLSTM_DH

GLA_DK_PAD = 128
GLA_DV_PAD = 256
GLA_QK_PAD = GLA_HEADS * GLA_DK_PAD
GLA_V_PAD = GLA_HEADS * GLA_DV_PAD
GATE_PAD = 128

V7X_VMEM_BYTES = 64 * 1024 * 1024
VMEM_LIMIT = V7X_VMEM_BYTES - 8 * 1024 * 1024

ROW_TILE = 512
HALO = 8
HALO_BF16 = 16


def _cparams(*sem):
    return pltpu.CompilerParams(dimension_semantics=sem, vmem_limit_bytes=VMEM_LIMIT)


def _rms(x, g):
    ms = jnp.mean(x * x, axis=-1, keepdims=True)
    return x * lax.rsqrt(ms + EPS) * g


def _split_bf16(x, n):
    parts = []
    r = x
    for _ in range(n):
        p = r.astype(BF16)
        parts.append(p)
        r = r - p.astype(F32)
    return parts


def _dot(a, b):
    return jnp.dot(a, b, preferred_element_type=F32)


def _dot_tn(a, b):
    return lax.dot_general(a, b, (((0,), (0,)), ((), ())), preferred_element_type=F32)


def _tri_masks(n, reverse):
    r = lax.broadcasted_iota(jnp.int32, (n, n), 0)
    c = lax.broadcasted_iota(jnp.int32, (n, n), 1)
    mask = (c >= r) if reverse else (c <= r)
    return mask, ((r >= c) if reverse else (r <= c)).astype(BF16)


def _cumsum_cols(x, tri_t_bf):
    acc = None
    for p in _split_bf16(x, 3):
        t = _dot(p, tri_t_bf)
        acc = t if acc is None else acc + t
    return acc


EVEN_SEGS = (FNET_WIDTH, GLA_QK_PAD, GLA_QK_PAD, GLA_V_PAD, GLA_V_PAD, GATE_PAD)
EVEN_W = sum(EVEN_SEGS)


def _inproj_even_kernel(h_ref, g_ref, w_ref, dft_ref, a_ref, b_ref, q_ref, k_ref, v_ref, r_ref, gg_ref):
    xn = _rms(h_ref[...], g_ref[...]).astype(BF16)
    offs = np.cumsum((0,) + EVEN_SEGS)

    def seg(i):
        return _dot(xn, w_ref[:, offs[i]:offs[i + 1]])

    u = seg(0).astype(BF16)
    ab = _dot(u, dft_ref[...])
    a_ref[...] = ab[:, :FNET_WIDTH].astype(BF16)
    b_ref[...] = ab[:, FNET_WIDTH:].astype(BF16)
    q_ref[...] = seg(1).astype(BF16)
    k_ref[...] = seg(2).astype(BF16)
    v_ref[...] = seg(3).astype(BF16)
    r_ref[...] = seg(4).astype(BF16)
    gg_ref[...] = seg(5)[:, :2 * GLA_GATE_RANK]


def _inproj_even(h2d, gain, w_pad, dft_cs, tm):
    t, d = h2d.shape
    row = lambda i: (i, 0)
    const = lambda i: (0, 0)
    widths = (FNET_WIDTH, FNET_WIDTH, GLA_QK_PAD, GLA_QK_PAD, GLA_V_PAD, GLA_V_PAD, 2 * GLA_GATE_RANK)
    dts = (BF16,) * 6 + (F32,)
    return pl.pallas_call(
        _inproj_even_kernel,
        grid=(t // tm,),
        in_specs=[pl.BlockSpec((tm, d), row), pl.BlockSpec((1, d), const),
                  pl.BlockSpec((d, EVEN_W), const), pl.BlockSpec((FNET_WIDTH, 2 * FNET_WIDTH), const)],
        out_specs=[pl.BlockSpec((tm, w), row) for w in widths],
        out_shape=[jax.ShapeDtypeStruct((t, w), dt) for w, dt in zip(widths, dts)],
        compiler_params=_cparams("parallel"),
        name="inproj_even",
    )(h2d, gain, w_pad, dft_cs)


ODD_GATES = 4 * MLSTM_HEADS
ODD_W = 3 * CONV_WIDTH + 4 * MLSTM_WIDTH + GATE_PAD


def _inproj_odd_kernel(h_ref, g_ref, w_ref, gb_ref, sb_ref, t_ref, q_ref, k_ref, v_ref, og_ref, gt_ref):
    xn = _rms(h_ref[...], g_ref[...]).astype(BF16)
    cw = CONV_WIDTH

    def seg(lo, n):
        return _dot(xn, w_ref[:, lo:lo + n])

    sb_ref[...] = seg(0, cw).astype(BF16)
    t_ref[...] = (seg(cw, cw) * seg(2 * cw, cw)).astype(BF16)
    base = 3 * cw
    q_ref[...] = seg(base, MLSTM_WIDTH).astype(BF16)
    k_ref[...] = seg(base + MLSTM_WIDTH, MLSTM_WIDTH).astype(BF16)
    v_ref[...] = seg(base + 2 * MLSTM_WIDTH, MLSTM_WIDTH).astype(BF16)
    og_ref[...] = seg(base + 3 * MLSTM_WIDTH, MLSTM_WIDTH).astype(BF16)
    gt_ref[...] = seg(base + 4 * MLSTM_WIDTH, GATE_PAD)[:, :ODD_GATES] + gb_ref[...]


def _inproj_odd(h2d, gain, w_pad, gate_bias, tm):
    t, d = h2d.shape
    row = lambda i: (i, 0)
    const = lambda i: (0, 0)
    widths = (CONV_WIDTH, CONV_WIDTH) + (MLSTM_WIDTH,) * 4 + (ODD_GATES,)
    dts = (BF16,) * 6 + (F32,)
    return pl.pallas_call(
        _inproj_odd_kernel,
        grid=(t // tm,),
        in_specs=[pl.BlockSpec((tm, d), row), pl.BlockSpec((1, d), const),
                  pl.BlockSpec((d, ODD_W), const), pl.BlockSpec((1, ODD_GATES), const)],
        out_specs=[pl.BlockSpec((tm, w), row) for w in widths],
        out_shape=[jax.ShapeDtypeStruct((t, w), dt) for w, dt in zip(widths, dts)],
        compiler_params=_cparams("parallel"),
        name="inproj_odd",
    )(h2d, gain, w_pad, gate_bias)


def _fnet_factors(s):
    lg = int(round(math.log2(s)))
    assert 2 ** lg == s
    n1 = 2 ** ((lg + 1) // 2)
    return n1, s // n1


@functools.lru_cache(maxsize=None)
def _fnet_tables(s):
    n1, n2 = _fnet_factors(s)
    k1 = np.arange(n1, dtype=np.int64)[None, :, None]
    m1 = np.arange(n1, dtype=np.int64)[None, None, :]
    j2 = np.arange(n2, dtype=np.int64)[:, None, None]
    ang = 2.0 * np.pi * ((k1 * (j2 + n2 * m1)) % s).astype(np.float64) / s
    gc = np.cos(ang) / np.sqrt(n1)
    gs = np.sin(ang) / np.sqrt(n1)
    g = np.concatenate([np.concatenate([gc, -gs], axis=2), np.concatenate([-gs, -gc], axis=2)], axis=1)
    k2 = np.arange(n2, dtype=np.int64)[:, None]
    m2 = np.arange(n2, dtype=np.int64)[None, :]
    ang2 = 2.0 * np.pi * ((k2 * m2) % n2).astype(np.float64) / n2
    f2 = np.concatenate([np.cos(ang2), np.sin(ang2)], axis=1) / np.sqrt(n2)
    return g.astype(np.float32).astype(BF16), f2.astype(np.float32).astype(BF16)


@functools.lru_cache(maxsize=None)
def _channel_dft():
    j = np.arange(FNET_WIDTH)
    same = (j[:, None] // FNET_GROUP_DIM) == (j[None, :] // FNET_GROUP_DIM)
    ang = 2.0 * np.pi * (((j[:, None] % FNET_GROUP_DIM) * (j[None, :] % FNET_GROUP_DIM)) % FNET_GROUP_DIM) / FNET_GROUP_DIM
    c = np.where(same, np.cos(ang), 0.0) / np.sqrt(FNET_GROUP_DIM)
    sn = np.where(same, np.sin(ang), 0.0) / np.sqrt(FNET_GROUP_DIM)
    return np.concatenate([c, sn], axis=1).astype(np.float32).astype(BF16)


def _fnet1_kernel(a_ref, b_ref, g_ref, p_ref, q_ref, *, n1, tn2, c):
    for t in range(tn2):
        sl = slice(t * c, (t + 1) * c)
        rhs = jnp.concatenate([a_ref[:, sl], b_ref[:, sl]], axis=0)
        res = _dot(g_ref[t], rhs)
        p_ref[:, sl] = res[:n1].astype(BF16)
        q_ref[:, sl] = res[n1:].astype(BF16)


def _fnet2_kernel(p_ref, q_ref, f_ref, y_ref, *, n2, tk, c):
    for t in range(tk):
        rs = slice(t * n2, (t + 1) * n2)
        rhs = jnp.concatenate([p_ref[rs, :], q_ref[rs, :]], axis=0)
        y_ref[:, t * c:(t + 1) * c] = _dot(f_ref[...], rhs).astype(BF16)


def _fnet(a, b):
    bsz, s, c = a.shape
    n1, n2 = _fnet_factors(s)
    g_tab, f2_tab = _fnet_tables(s)
    tn2 = min(8, n2)
    tk = min(8, n1)
    av = a.reshape(bsz, n1, n2 * c)
    bv = b.reshape(bsz, n1, n2 * c)
    blk = pl.BlockSpec((None, n1, tn2 * c), lambda j, i: (i, 0, j))
    p, q = pl.pallas_call(
        functools.partial(_fnet1_kernel, n1=n1, tn2=tn2, c=c),
        grid=(n2 // tn2, bsz),
        in_specs=[blk, blk, pl.BlockSpec((tn2, 2 * n1, 2 * n1), lambda j, i: (j, 0, 0))],
        out_specs=[blk, blk],
        out_shape=[jax.ShapeDtypeStruct((bsz, n1, n2 * c), BF16)] * 2,
        compiler_params=_cparams("parallel", "parallel"),
        name="fnet_stage1",
    )(av, bv, g_tab)
    pv = p.reshape(bsz, n1 * n2, c)
    qv = q.reshape(bsz, n1 * n2, c)
    rblk = pl.BlockSpec((None, tk * n2, c), lambda i, j: (i, j, 0))
    y = pl.pallas_call(
        functools.partial(_fnet2_kernel, n2=n2, tk=tk, c=c),
        grid=(bsz, n1 // tk),
        in_specs=[rblk, rblk, pl.BlockSpec((n2, 2 * n2), lambda i, j: (0, 0))],
        out_specs=pl.BlockSpec((None, n2, tk * c), lambda i, j: (i, 0, j)),
        out_shape=jax.ShapeDtypeStruct((bsz, n2, n1 * c), BF16),
        compiler_params=_cparams("parallel", "parallel"),
        name="fnet_stage2",
    )(pv, qv, f2_tab)
    return y.reshape(bsz, s, c)


GCHUNK = 128
GHALF = GCHUNK // 2


@functools.lru_cache(maxsize=None)
def _gla_cum_tables(reverse):
    n = GCHUNK
    i = np.arange(n)[:, None]
    t = np.arange(n)[None, :]
    if reverse:
        tri = (t >= i)
        mid = (t >= GHALF)
        rest = (t < i)
    else:
        tri = (t <= i)
        mid = (t <= GHALF - 1)
        rest = (t > i)
    tri_n = tri.astype(np.float32)
    key = np.concatenate([(mid.astype(np.float32) - tri).T, rest.astype(np.float32).T,
                          np.ones((n, n), np.float32)], axis=1)
    return tri_n.astype(BF16), key.astype(BF16)


def _dot_split2(a, b, split_lhs):
    if split_lhs:
        p = _split_bf16(a, 2)
        return _dot(p[0], b) + _dot(p[1], b)
    p = _split_bf16(b, 2)
    return _dot(a, p[0]) + _dot(a, p[1])


def _gla_scan(q_ref, kt_ref, v_ref, g_ref, gt_ref, w2_ref, w2t_ref, b_ref, bt_ref, tri_ref, key_ref,
              o_ref, state_ref, reverse):
    ts = q_ref.shape[0]
    n = GCHUNK
    nck = ts // n
    dk, dv = GLA_DK_PAD, GLA_DV_PAD
    r = lax.broadcasted_iota(jnp.int32, (n, n), 0)
    c = lax.broadcasted_iota(jnp.int32, (n, n), 1)
    mask = (c >= r) if reverse else (c <= r)
    w2 = w2_ref[...].astype(BF16)
    w2t = w2t_ref[...].astype(BF16)
    tri = tri_ref[...]
    key = key_ref[...]
    mid_row = GHALF if reverse else GHALF - 1

    @pl.when(pl.program_id(1) == 0)
    def _():
        state_ref[...] = jnp.zeros_like(state_ref)

    for ci in range(nck):
        ch = nck - 1 - ci if reverse else ci
        rows = slice(ch * n, (ch + 1) * n)
        la = jax.nn.log_sigmoid(_dot(g_ref[rows, :].astype(BF16), w2) + b_ref[...]) * (1.0 / GLA_TAU)
        cum = _dot_split2(tri, la, False)
        qf = q_ref[rows, :].astype(F32) * (GLA_DK ** -0.5)
        q_mid = (qf * jnp.exp(cum - cum[mid_row:mid_row + 1, :])).astype(BF16)
        q_in = (qf * jnp.exp(cum)).astype(BF16)
        lat = jax.nn.log_sigmoid(_dot(w2t, gt_ref[:, rows].astype(BF16)) + bt_ref[...]) * (1.0 / GLA_TAU)
        rel = _dot_split2(lat, key, True)
        ktf = kt_ref[:, rows].astype(F32)
        k_mid = (ktf * jnp.exp(rel[:, :n])).astype(BF16)
        k_out = (ktf * jnp.exp(rel[:, n:2 * n])).astype(BF16)
        dec = jnp.exp(rel[:, 2 * n:])
        for h in range(GLA_HEADS):
            ks = slice(h * dk, (h + 1) * dk)
            vs = slice(h * dv, (h + 1) * dv)
            vh = v_ref[rows, vs]
            att = jnp.where(mask, _dot(q_mid[:, ks], k_mid[ks, :]), 0.0).astype(BF16)
            st = state_ref[h]
            o_ref[rows, vs] = _dot(jnp.concatenate([att, q_in[:, ks]], axis=1),
                                   jnp.concatenate([vh, st.astype(BF16)], axis=0))
            dech = dec[ks, :]
            state_ref[h] = st * jnp.concatenate([dech] * (dv // n), axis=1) + _dot(k_out[ks, :], vh)


def _gla_fwd_kernel(q_ref, kt_ref, v_ref, g_ref, gt_ref, w2_ref, w2t_ref, b_ref, bt_ref, tri_ref, key_ref,
                    of_ref, state_ref, o_scr):
    _gla_scan(q_ref, kt_ref, v_ref, g_ref, gt_ref, w2_ref, w2t_ref, b_ref, bt_ref, tri_ref, key_ref,
              o_scr, state_ref, False)
    of_ref[...] = o_scr[...].astype(BF16)


def _gla_bwd_kernel(q_ref, kt_ref, v_ref, g_ref, gt_ref, w2_ref, w2t_ref, b_ref, bt_ref, tri_ref, key_ref,
                    of_ref, r_ref, yf_ref, h_ref, gn_ref, wf_ref, wg_ref, out_ref, state_ref, o_scr):
    _gla_scan(q_ref, kt_ref, v_ref, g_ref, gt_ref, w2_ref, w2t_ref, b_ref, bt_ref, tri_ref, key_ref,
              o_scr, state_ref, True)
    o = o_scr[...] + of_ref[...].astype(F32)
    normed = []
    for h in range(GLA_HEADS):
        oh = o[:, h * GLA_DV_PAD:(h + 1) * GLA_DV_PAD]
        ms = jnp.sum(oh * oh, axis=-1, keepdims=True) * (1.0 / GLA_DV)
        normed.append(oh * lax.rsqrt(ms + EPS))
    yg = jnp.concatenate(normed, axis=1) * gn_ref[...]
    yg = (yg * jax.nn.silu(r_ref[...].astype(F32))).astype(BF16)
    out_ref[...] = h_ref[...] + _dot(yf_ref[...], wf_ref[...]) + _dot(yg, wg_ref[...])


def _gla(q, kt, v, g, gt, w2f, bf, w2b, bb, r, yf, h, gnorm, w_out_f, w_out_g, ts):
    bsz, s, _ = q.shape
    d = h.shape[-1]
    nblk = s // ts
    gw = 2 * GLA_GATE_RANK
    fwd = lambda b, i: (b, i, 0)
    bwd = lambda b, i: (b, nblk - 1 - i, 0)
    const = lambda b, i: (0, 0)

    def seq_specs(rev):
        blk = (lambda i: nblk - 1 - i) if rev else (lambda i: i)
        im = lambda b, i: (b, blk(i), 0)
        imt = lambda b, i: (b, 0, blk(i))
        return [pl.BlockSpec((None, ts, GLA_QK_PAD), im), pl.BlockSpec((None, GLA_QK_PAD, ts), imt),
                pl.BlockSpec((None, ts, GLA_V_PAD), im), pl.BlockSpec((None, ts, gw), im),
                pl.BlockSpec((None, gw, ts), imt),
                pl.BlockSpec((gw, GLA_QK_PAD), const), pl.BlockSpec((GLA_QK_PAD, gw), const),
                pl.BlockSpec((1, GLA_QK_PAD), const), pl.BlockSpec((GLA_QK_PAD, GCHUNK), const),
                pl.BlockSpec((GCHUNK, GCHUNK), const), pl.BlockSpec((GCHUNK, 3 * GCHUNK), const)]

    def gate_args(w2, bias, rev):
        tri, key = _gla_cum_tables(rev)
        return (w2, w2.T, bias, jnp.broadcast_to(bias.T, (GLA_QK_PAD, GCHUNK)), tri, key)

    scratch = [pltpu.VMEM((GLA_HEADS, GLA_DK_PAD, GLA_DV_PAD), F32), pltpu.VMEM((ts, GLA_V_PAD), F32)]
    o_f = pl.pallas_call(
        _gla_fwd_kernel,
        grid=(bsz, nblk),
        in_specs=seq_specs(False),
        out_specs=pl.BlockSpec((None, ts, GLA_V_PAD), fwd),
        out_shape=jax.ShapeDtypeStruct((bsz, s, GLA_V_PAD), BF16),
        scratch_shapes=scratch,
        compiler_params=_cparams("parallel", "arbitrary"),
        name="gla_fwd",
    )(q, kt, v, g, gt, *gate_args(w2f, bf, False))
    return pl.pallas_call(
        _gla_bwd_kernel,
        grid=(bsz, nblk),
        in_specs=seq_specs(True) + [
            pl.BlockSpec((None, ts, GLA_V_PAD), bwd), pl.BlockSpec((None, ts, GLA_V_PAD), bwd),
            pl.BlockSpec((None, ts, FNET_WIDTH), bwd), pl.BlockSpec((None, ts, d), bwd),
            pl.BlockSpec((1, GLA_V_PAD), const), pl.BlockSpec((FNET_WIDTH, d), const),
            pl.BlockSpec((GLA_V_PAD, d), const)],
        out_specs=pl.BlockSpec((None, ts, d), bwd),
        out_shape=jax.ShapeDtypeStruct((bsz, s, d), F32),
        scratch_shapes=scratch,
        compiler_params=_cparams("parallel", "arbitrary"),
        name="gla_bwd_out",
    )(q, kt, v, g, gt, *gate_args(w2b, bb, True), o_f, r, yf, h, gnorm, w_out_f, w_out_g)


MCHUNK = 128
assert MCHUNK == MLSTM_DH
MGATE_ROWS = 2 * MLSTM_HEADS
ZROWS = 80
ZCOLS = 3 * MLSTM_DH


@functools.lru_cache(maxsize=None)
def _mlstm_zsel(reverse):
    z = np.zeros((ZROWS, MLSTM_HEADS * ZCOLS), np.float32)
    rbase = MLSTM_HEADS if reverse else 0
    for h in range(MLSTM_HEADS):
        for t in range(3):
            for k in range(3):
                z[(3 * t + k) * MGATE_ROWS + rbase + h, h * ZCOLS + t * MLSTM_DH:h * ZCOLS + (t + 1) * MLSTM_DH] = 1.0
    return z.astype(BF16)


def _cummax_lanes(x, reverse):
    n = x.shape[-1]
    lane = lax.broadcasted_iota(jnp.int32, x.shape, 1)
    s = 1
    while s < n:
        if reverse:
            shifted = jnp.where(lane < n - s, pltpu.roll(x, n - s, axis=1), -jnp.inf)
        else:
            shifted = jnp.where(lane >= s, pltpu.roll(x, s, axis=1), -jnp.inf)
        x = jnp.maximum(x, shifted)
        s *= 2
    return x


def _mlstm_scan(q_ref, kt_ref, v_ref, gi_ref, gf_ref, zsel_ref, o_ref, st_ref, m_ref, reverse):
    ts = q_ref.shape[0]
    n = MCHUNK
    nck = ts // n
    dh = MLSTM_DH
    gr = MGATE_ROWS
    scale = dh ** -0.5
    mask, tri_t_bf = _tri_masks(n, reverse)
    rbase = MLSTM_HEADS if reverse else 0

    @pl.when(pl.program_id(1) == 0)
    def _():
        st_ref[...] = jnp.zeros_like(st_ref)
        m_ref[...] = jnp.zeros_like(m_ref)

    gi = gi_ref[...].reshape(nck * gr, n)
    gf = gf_ref[...].reshape(nck * gr, n)
    cum = _cumsum_cols(jax.nn.log_sigmoid(gf), tri_t_bf)
    b = gi - cum
    md = cum + _cummax_lanes(b, reverse)
    tot = cum[:, 0:1] if reverse else cum[:, n - 1:n]
    w_end = tot + b
    m_loc = jnp.max(w_end, axis=1, keepdims=True)
    e = jnp.exp(w_end - m_loc)
    ones = jnp.ones((n, dh), BF16)
    zpad = jnp.zeros((gr, n), BF16)
    zsel = zsel_ref[...]

    for ci in range(nck):
        c = nck - 1 - ci if reverse else ci
        rs = slice(c * gr, (c + 1) * gr)
        rows = slice(c * n, (c + 1) * n)
        m = m_ref[...]
        lw = cum[rs] + m[:, :n]
        m_t = jnp.maximum(lw, md[rs])
        pieces = _split_bf16(cum[rs] - m_t, 3) + _split_bf16(lw - m_t, 3) + _split_bf16(-m_t, 3) + [zpad]
        z = _dot_tn(jnp.concatenate(pieces, axis=0), zsel)
        m_new = jnp.maximum(tot[rs] + m, m_loc[rs])
        fa = jnp.exp(tot[rs] + m - m_new)
        fb = jnp.exp(m_loc[rs] - m_new)
        m_ref[...] = m_new
        for h in range(MLSTM_HEADS):
            r = c * gr + rbase + h
            hs = slice(h * dh, (h + 1) * dh)
            zb = h * ZCOLS
            qh = q_ref[rows, hs]
            kth = kt_ref[hs, rows]
            vaug = jnp.concatenate([v_ref[rows, hs], ones], axis=1)
            expo = jnp.where(mask, z[:, zb:zb + dh] + b[r:r + 1, :], -jnp.inf)
            s_qk = (_dot(qh, kth) * (jnp.exp(expo) * scale)).astype(BF16)
            qa = (qh.astype(F32) * (jnp.exp(z[:, zb + dh:zb + 2 * dh]) * scale)).astype(BF16)
            st = st_ref[h]
            res = _dot(jnp.concatenate([s_qk, qa], axis=1), jnp.concatenate([vaug, st.astype(BF16)], axis=0))
            floor = jnp.exp(z[:, zb + 2 * dh:zb + 3 * dh])
            o_ref[rows, hs] = res[:, :dh] / jnp.maximum(jnp.abs(res[:, dh:]), floor)
            ket = (kth.astype(F32) * e[r:r + 1, :]).astype(BF16)
            rr = rbase + h
            st_ref[h] = fa[rr:rr + 1, :] * st + fb[rr:rr + 1, :] * _dot(ket, vaug)


def _mlstm_fwd_kernel(q_ref, kt_ref, v_ref, gi_ref, gf_ref, zsel_ref, hf_ref, st_ref, m_ref, o_scr):
    _mlstm_scan(q_ref, kt_ref, v_ref, gi_ref, gf_ref, zsel_ref, o_scr, st_ref, m_ref, False)
    hf_ref[...] = o_scr[...].astype(BF16)


def _mlstm_bwd_kernel(q_ref, kt_ref, v_ref, gi_ref, gf_ref, zsel_ref, hf_ref, og_ref, sb_ref, t_ref, tp_ref, tn_ref,
                      h_ref, mn_ref, cw_ref, cb_ref, wc_ref, wm_ref, out_ref, st_ref, m_ref, o_scr):
    _mlstm_scan(q_ref, kt_ref, v_ref, gi_ref, gf_ref, zsel_ref, o_scr, st_ref, m_ref, True)
    ts = q_ref.shape[0]
    o = o_scr[...] + hf_ref[...].astype(F32)
    normed = []
    for h in range(MLSTM_HEADS):
        oh = o[:, h * MLSTM_DH:(h + 1) * MLSTM_DH]
        normed.append(oh * lax.rsqrt(jnp.mean(oh * oh, axis=-1, keepdims=True) + EPS))
    ym = jnp.concatenate(normed, axis=1) * mn_ref[...]
    ym = (ym * jax.nn.sigmoid(og_ref[...].astype(F32))).astype(BF16)
    i = pl.program_id(1)
    nblk = pl.num_programs(1)
    prev = jnp.where(i < nblk - 1, tp_ref[...].astype(F32), 0.0)
    nxt = jnp.where(i > 0, tn_ref[...].astype(F32), 0.0)
    ext = jnp.concatenate([prev, t_ref[...].astype(F32), nxt], axis=0)
    rows = ts + 2 * HALO_BF16
    lo = pltpu.roll(ext, 1, axis=0)[HALO_BF16:HALO_BF16 + ts]
    hi = pltpu.roll(ext, rows - 1, axis=0)[HALO_BF16:HALO_BF16 + ts]
    cw = cw_ref[...]
    conv = lo * cw[0:1, :] + ext[HALO_BF16:HALO_BF16 + ts] * cw[1:2, :] + hi * cw[2:3, :] + cb_ref[...]
    yc = (sb_ref[...].astype(F32) * conv).astype(BF16)
    out_ref[...] = h_ref[...] + _dot(yc, wc_ref[...]) + _dot(ym, wm_ref[...])


def _mlstm(q, kt, v, gi, gf, og, sb, t, h, mnorm, conv_w, conv_b, w_out_c, w_out_m, ts):
    bsz, s, _ = q.shape
    d = h.shape[-1]
    nblk = s // ts
    nck = ts // MCHUNK
    hb = ts // HALO_BF16
    fwd = lambda b, i: (b, i, 0)
    bwd = lambda b, i: (b, nblk - 1 - i, 0)
    const = lambda b, i: (0, 0)

    def seq_specs(rev):
        w = MLSTM_WIDTH
        blk = (lambda i: nblk - 1 - i) if rev else (lambda i: i)
        im = lambda b, i: (b, blk(i), 0)
        imt = lambda b, i: (b, 0, blk(i))
        im4 = lambda b, i: (b, blk(i), 0, 0)
        gspec = pl.BlockSpec((None, nck, MGATE_ROWS, MCHUNK), im4)
        return [pl.BlockSpec((None, ts, w), im), pl.BlockSpec((None, w, ts), imt), pl.BlockSpec((None, ts, w), im),
                gspec, gspec, pl.BlockSpec((ZROWS, MLSTM_HEADS * ZCOLS), const)]

    scratch = [pltpu.VMEM((MLSTM_HEADS, MLSTM_DH, 2 * MLSTM_DH), F32), pltpu.VMEM((MGATE_ROWS, 2 * MLSTM_DH), F32),
               pltpu.VMEM((ts, MLSTM_WIDTH), F32)]
    h_f = pl.pallas_call(
        _mlstm_fwd_kernel,
        grid=(bsz, nblk),
        in_specs=seq_specs(False),
        out_specs=pl.BlockSpec((None, ts, MLSTM_WIDTH), fwd),
        out_shape=jax.ShapeDtypeStruct((bsz, s, MLSTM_WIDTH), BF16),
        scratch_shapes=scratch,
        compiler_params=_cparams("parallel", "arbitrary"),
        name="mlstm_fwd",
    )(q, kt, v, gi, gf, _mlstm_zsel(False))
    last16 = s // HALO_BF16 - 1
    halo_prev = lambda b, i: (b, jnp.maximum((nblk - 1 - i) * hb - 1, 0), 0)
    halo_next = lambda b, i: (b, jnp.minimum((nblk - i) * hb, last16), 0)
    return pl.pallas_call(
        _mlstm_bwd_kernel,
        grid=(bsz, nblk),
        in_specs=seq_specs(True) + [
            pl.BlockSpec((None, ts, MLSTM_WIDTH), bwd), pl.BlockSpec((None, ts, MLSTM_WIDTH), bwd),
            pl.BlockSpec((None, ts, CONV_WIDTH), bwd), pl.BlockSpec((None, ts, CONV_WIDTH), bwd),
            pl.BlockSpec((None, HALO_BF16, CONV_WIDTH), halo_prev),
            pl.BlockSpec((None, HALO_BF16, CONV_WIDTH), halo_next),
            pl.BlockSpec((None, ts, d), bwd),
            pl.BlockSpec((1, MLSTM_WIDTH), const), pl.BlockSpec((3, CONV_WIDTH), const),
            pl.BlockSpec((1, CONV_WIDTH), const), pl.BlockSpec((CONV_WIDTH, d), const),
            pl.BlockSpec((MLSTM_WIDTH, d), const)],
        out_specs=pl.BlockSpec((None, ts, d), bwd),
        out_shape=jax.ShapeDtypeStruct((bsz, s, d), F32),
        scratch_shapes=scratch,
        compiler_params=_cparams("parallel", "arbitrary"),
        name="mlstm_bwd_out",
    )(q, kt, v, gi, gf, _mlstm_zsel(True), h_f, og, sb, t, t, t, h, mnorm, conv_w, conv_b, w_out_c, w_out_m)


def _ffn_kernel(h_ref, hp_ref, hn_ref, p_ref, fg_ref, wu_ref, cw_ref, cb_ref, wd_ref, pg_ref, pgw_ref, pw_ref,
                fn_ref, out_ref, xn_scr, *, d_ff, f_tile, final):
    ts = h_ref.shape[0]
    i = pl.program_id(1)
    nblk = pl.num_programs(1)
    x = h_ref[...]
    fg = fg_ref[...]
    xn_scr[0:HALO, :] = jnp.where(i > 0, _rms(hp_ref[...], fg), 0.0).astype(BF16)
    xn_scr[HALO:HALO + ts, :] = _rms(x, fg).astype(BF16)
    xn_scr[HALO + ts:, :] = jnp.where(i < nblk - 1, _rms(hn_ref[...], fg), 0.0).astype(BF16)
    rows = ts + 2 * HALO
    xe = xn_scr[...]
    xc = xe[HALO:HALO + ts]
    acc = jnp.zeros_like(x)
    for f0 in range(0, d_ff, f_tile):
        gate = _dot(xe, wu_ref[:, f0:f0 + f_tile])
        val = _dot(xc, wu_ref[:, d_ff + f0:d_ff + f0 + f_tile])
        cw = cw_ref[:, f0:f0 + f_tile]
        lo = pltpu.roll(gate, 1, axis=0)[HALO:HALO + ts]
        hi = pltpu.roll(gate, rows - 1, axis=0)[HALO:HALO + ts]
        conv = (lo * cw[0:1, :] + gate[HALO:HALO + ts] * cw[1:2, :] + hi * cw[2:3, :]
                + cb_ref[:, f0:f0 + f_tile])
        act = (jax.nn.silu(conv) * val).astype(BF16)
        acc = acc + _dot(act, wd_ref[f0:f0 + f_tile, :])
    h2 = x + acc
    gate = jax.nn.sigmoid(_dot(_rms(h2, pg_ref[...]).astype(BF16), pgw_ref[...]))
    h3 = h2 + _dot(p_ref[...].astype(BF16), pw_ref[...]) * gate
    out_ref[...] = _rms(h3, fn_ref[...]) if final else h3


def _ffn(h, p, layer, fgain, w_up, conv_w, conv_b, w_down, pgain, pg_w, p_w, final_gain, final, ts, f_tile):
    bsz, s, d = h.shape
    d_ff = w_down.shape[0]
    nblk = s // ts
    hb = ts // HALO
    last8 = s // HALO - 1
    blk = lambda b, i: (b, i, 0)
    const = lambda b, i: (0, 0)
    once = pl.Buffered(1)

    def wspec(shape):
        return pl.BlockSpec(shape, const, pipeline_mode=once)

    return pl.pallas_call(
        functools.partial(_ffn_kernel, d_ff=d_ff, f_tile=f_tile, final=final),
        grid=(bsz, nblk),
        in_specs=[pl.BlockSpec((None, ts, d), blk),
                  pl.BlockSpec((None, HALO, d), lambda b, i: (b, jnp.maximum(i * hb - 1, 0), 0)),
                  pl.BlockSpec((None, HALO, d), lambda b, i: (b, jnp.minimum((i + 1) * hb, last8), 0)),
                  pl.BlockSpec((None, None, ts, p.shape[-1]), lambda b, i: (layer, b, i, 0)),
                  wspec((1, d)), wspec((d, 2 * d_ff)), wspec((3, d_ff)), wspec((1, d_ff)), wspec((d_ff, d)),
                  wspec((1, d)), wspec((d, d)), wspec((p.shape[-1], d)), wspec((1, d))],
        out_specs=pl.BlockSpec((None, ts, d), blk),
        out_shape=jax.ShapeDtypeStruct((bsz, s, d), F32),
        scratch_shapes=[pltpu.VMEM((ts + 2 * HALO, d), BF16)],
        compiler_params=_cparams("parallel", "parallel"),
        name="ffn_ple",
    )(h, h, h, p, fgain, w_up, conv_w, conv_b, w_down, pgain, pg_w, p_w, final_gain)


def _pad_heads_cols(w, heads, dh, dpad):
    lead = w.shape[:-1]
    w = w.reshape(lead + (heads, dh))
    w = jnp.pad(w, [(0, 0)] * len(lead) + [(0, 0), (0, dpad - dh)])
    return w.reshape(lead + (heads * dpad,))


def _pad_cols(w, width):
    return jnp.pad(w, [(0, 0)] * (w.ndim - 1) + [(0, width - w.shape[-1])])


def _even_weights(w_in, w2_f, b_f, w2_b, b_b, gnorm, w_out):
    cuts = np.cumsum([FNET_WIDTH, GLA_HEADS * GLA_DK, GLA_HEADS * GLA_DK, GLA_HEADS * GLA_DV,
                      GLA_HEADS * GLA_DV]).tolist()
    wu, wq, wk, wv, wr, wg = jnp.split(w_in, cuts, axis=-1)
    w_pad = jnp.concatenate([
        wu, _pad_heads_cols(wq, GLA_HEADS, GLA_DK, GLA_DK_PAD), _pad_heads_cols(wk, GLA_HEADS, GLA_DK, GLA_DK_PAD),
        _pad_heads_cols(wv, GLA_HEADS, GLA_DV, GLA_DV_PAD), _pad_heads_cols(wr, GLA_HEADS, GLA_DV, GLA_DV_PAD),
        _pad_cols(wg, GATE_PAD)], axis=-1).astype(BF16)
    zero = jnp.zeros((GLA_GATE_RANK, GLA_QK_PAD), F32)
    w2f = jnp.concatenate([_pad_heads_cols(w2_f, GLA_HEADS, GLA_DK, GLA_DK_PAD), zero], axis=0)
    w2b = jnp.concatenate([zero, _pad_heads_cols(w2_b, GLA_HEADS, GLA_DK, GLA_DK_PAD)], axis=0)
    bf = _pad_heads_cols(b_f[None, :], GLA_HEADS, GLA_DK, GLA_DK_PAD)
    bb = _pad_heads_cols(b_b[None, :], GLA_HEADS, GLA_DK, GLA_DK_PAD)
    gn = _pad_heads_cols(gnorm[None, :], GLA_HEADS, GLA_DV, GLA_DV_PAD)
    w_out_f = w_out[:FNET_WIDTH].astype(BF16)
    wog = w_out[FNET_WIDTH:].reshape(GLA_HEADS, GLA_DV, -1)
    wog = jnp.pad(wog, ((0, 0), (0, GLA_DV_PAD - GLA_DV), (0, 0))).reshape(GLA_V_PAD, -1).astype(BF16)
    return w_pad, w2f, bf, w2b, bb, gn, w_out_f, wog


def _trunk(x, p, e_norm, e_w_in, e_gla_w2_f, e_gla_b_f, e_gla_w2_b, e_gla_b_b, e_gla_norm, e_w_out,
           o_norm, o_w_in, o_conv_w, o_conv_b, o_gate_bias, o_mlstm_norm, o_w_out,
           ffn_norm, ffn_w_up, ffn_conv_w, ffn_conv_b, ffn_w_down,
           ple_w, ple_gate_norm, ple_gate_w, final_norm, *, ts=ROW_TILE, f_tile=256):
    bsz, s, d = x.shape
    depth = p.shape[0]
    t = bsz * s
    ts = min(ts, s)
    h = x
    for layer in range(depth):
        j = layer // 2
        h2d = h.reshape(t, d)
        if layer % 2 == 0:
            w_pad, w2f, bf, w2b, bb, gn, w_out_f, w_out_g = _even_weights(
                e_w_in[j], e_gla_w2_f[j], e_gla_b_f[j], e_gla_w2_b[j], e_gla_b_b[j], e_gla_norm[j], e_w_out[j])
            a, b, q, k, v, r, g = _inproj_even(h2d, e_norm[j][None, :], w_pad, _channel_dft(), ts)
            sh = lambda z: z.reshape(bsz, s, z.shape[-1])
            yf = _fnet(sh(a), sh(b))
            g = sh(g)
            h = _gla(sh(q), sh(k).transpose(0, 2, 1), sh(v), g, g.transpose(0, 2, 1), w2f, bf, w2b, bb,
                     sh(r), yf, h, gn, w_out_f, w_out_g, ts)
        else:
            w_pad = _pad_cols(o_w_in[j], ODD_W).astype(BF16)
            sb, tt, q, k, v, og, gates = _inproj_odd(h2d, o_norm[j][None, :], w_pad, o_gate_bias[j][None, :], ts)
            sh = lambda z: z.reshape(bsz, s, z.shape[-1])
            nh = MLSTM_HEADS
            gates = gates.reshape(bsz, s // MCHUNK, MCHUNK, ODD_GATES).transpose(0, 1, 3, 2)
            gi = jnp.concatenate([gates[:, :, 0:nh], gates[:, :, 2 * nh:3 * nh]], axis=2)
            gf = jnp.concatenate([gates[:, :, nh:2 * nh], gates[:, :, 3 * nh:4 * nh]], axis=2)
            w_out = o_w_out[j].astype(BF16)
            h = _mlstm(sh(q), sh(k).transpose(0, 2, 1), sh(v), gi, gf, sh(og), sh(sb), sh(tt), h,
                       o_mlstm_norm[j][None, :], o_conv_w[j], o_conv_b[j][None, :],
                       w_out[:CONV_WIDTH], w_out[CONV_WIDTH:], ts)
        h = _ffn(h, p, layer, ffn_norm[layer][None, :], ffn_w_up[layer].astype(BF16), ffn_conv_w[layer],
                 ffn_conv_b[layer][None, :], ffn_w_down[layer].astype(BF16), ple_gate_norm[layer][None, :],
                 ple_gate_w[layer].astype(BF16), ple_w[layer].astype(BF16), final_norm[None, :],
                 layer == depth - 1, ts, f_tile)
    return h


def kernel(x_prompt, x_sample, p_prompt, p_sample, e_norm, e_w_in, e_gla_w2_f, e_gla_b_f, e_gla_w2_b, e_gla_b_b, e_gla_norm, e_w_out, o_norm, o_w_in, o_conv_w, o_conv_b, o_gate_bias, o_mlstm_norm, o_w_out, ffn_norm, ffn_w_up, ffn_conv_w, ffn_conv_b, ffn_w_down, ple_w, ple_gate_norm, ple_gate_w, final_norm):
    weights = (e_norm, e_w_in, e_gla_w2_f, e_gla_b_f, e_gla_w2_b, e_gla_b_b, e_gla_norm, e_w_out,
               o_norm, o_w_in, o_conv_w, o_conv_b, o_gate_bias, o_mlstm_norm, o_w_out,
               ffn_norm, ffn_w_up, ffn_conv_w, ffn_conv_b, ffn_w_down,
               ple_w, ple_gate_norm, ple_gate_w, final_norm)
    return (_trunk(x_prompt, p_prompt, *weights), _trunk(x_sample, p_sample, *weights))
```

```python
import functools
import math

import numpy as np
import jax
import jax.numpy as jnp
from jax import lax
from jax.experimental import pallas as pl
from jax.experimental.pallas import tpu as pltpu

F32 = jnp.float32
BF16 = jnp.bfloat16
EPS = 1e-6

FNET_GROUPS = 4
FNET_GROUP_DIM = 64
FNET_WIDTH = FNET_GROUPS * FNET_GROUP_DIM
GLA_HEADS = 4
GLA_DK = 96
GLA_DV = 192
GLA_GATE_RANK = 16
GLA_TAU = 16.0
CONV_WIDTH = 512
MLSTM_HEADS = 4
MLSTM_DH = 128
MLSTM_WIDTH = MLSTM_HEADS * MLSTM_DH

GLA_DK_PAD = 128
GLA_DV_PAD = 256
GLA_QK_PAD = GLA_HEADS * GLA_DK_PAD
GLA_V_PAD = GLA_HEADS * GLA_DV_PAD
GATE_PAD = 128

V7X_VMEM_BYTES = 64 * 1024 * 1024
VMEM_LIMIT = V7X_VMEM_BYTES - 8 * 1024 * 1024

ROW_TILE = 512
HALO = 8
HALO_BF16 = 16


def _cparams(*sem):
    return pltpu.CompilerParams(dimension_semantics=sem, vmem_limit_bytes=VMEM_LIMIT)


def _rms(x, g):
    ms = jnp.mean(x * x, axis=-1, keepdims=True)
    return x * lax.rsqrt(ms + EPS) * g


def _split_bf16(x, n):
    parts = []
    r = x
    for _ in range(n):
        p = r.astype(BF16)
        parts.append(p)
        r = r - p.astype(F32)
    return parts


def _dot(a, b):
    return jnp.dot(a, b, preferred_element_type=F32)


def _dot_tn(a, b):
    return lax.dot_general(a, b, (((0,), (0,)), ((), ())), preferred_element_type=F32)


def _tri_masks(n, reverse):
    r = lax.broadcasted_iota(jnp.int32, (n, n), 0)
    c = lax.broadcasted_iota(jnp.int32, (n, n), 1)
    mask = (c >= r) if reverse else (c <= r)
    return mask, ((r >= c) if reverse else (r <= c)).astype(BF16)


def _cumsum_cols(x, tri_t_bf):
    acc = None
    for p in _split_bf16(x, 3):
        t = _dot(p, tri_t_bf)
        acc = t if acc is None else acc + t
    return acc


INPROJ_SUB = 512
EVEN_SEGS = (FNET_WIDTH, GLA_QK_PAD, GLA_QK_PAD, GLA_V_PAD, GLA_V_PAD, GATE_PAD)
EVEN_W = sum(EVEN_SEGS)


def _inproj_even_kernel(h_ref, g_ref, w_ref, dft_ref, a_ref, b_ref, q_ref, k_ref, v_ref, r_ref, gg_ref):
    offs = np.cumsum((0,) + EVEN_SEGS)
    tm = h_ref.shape[0]
    for r0 in range(0, tm, INPROJ_SUB):
        rows = slice(r0, r0 + INPROJ_SUB)
        xn = _rms(h_ref[rows, :], g_ref[...]).astype(BF16)

        def seg(i):
            return _dot(xn, w_ref[:, offs[i]:offs[i + 1]])

        u = seg(0).astype(BF16)
        ab = _dot(u, dft_ref[...])
        a_ref[rows, :] = ab[:, :FNET_WIDTH].astype(BF16)
        b_ref[rows, :] = ab[:, FNET_WIDTH:].astype(BF16)
        q_ref[rows, :] = seg(1).astype(BF16)
        k_ref[rows, :] = seg(2).astype(BF16)
        v_ref[rows, :] = seg(3).astype(BF16)
        r_ref[rows, :] = seg(4).astype(BF16)
        gg_ref[rows, :] = seg(5)[:, :2 * GLA_GATE_RANK]


def _inproj_even(h2d, gain, w_pad, dft_cs, tm):
    t, d = h2d.shape
    row = lambda i: (i, 0)
    const = lambda i: (0, 0)
    widths = (FNET_WIDTH, FNET_WIDTH, GLA_QK_PAD, GLA_QK_PAD, GLA_V_PAD, GLA_V_PAD, 2 * GLA_GATE_RANK)
    dts = (BF16,) * 6 + (F32,)
    return pl.pallas_call(
        _inproj_even_kernel,
        grid=(t // tm,),
        in_specs=[pl.BlockSpec((tm, d), row), pl.BlockSpec((1, d), const),
                  pl.BlockSpec((d, EVEN_W), const), pl.BlockSpec((FNET_WIDTH, 2 * FNET_WIDTH), const)],
        out_specs=[pl.BlockSpec((tm, w), row) for w in widths],
        out_shape=[jax.ShapeDtypeStruct((t, w), dt) for w, dt in zip(widths, dts)],
        compiler_params=_cparams("parallel"),
        name="inproj_even",
    )(h2d, gain, w_pad, dft_cs)


ODD_GATES = 4 * MLSTM_HEADS
ODD_W = 3 * CONV_WIDTH + 4 * MLSTM_WIDTH + GATE_PAD


def _inproj_odd_kernel(h_ref, g_ref, w_ref, gb_ref, sb_ref, t_ref, q_ref, k_ref, v_ref, og_ref, gt_ref):
    cw = CONV_WIDTH
    base = 3 * cw
    tm = h_ref.shape[0]
    for r0 in range(0, tm, INPROJ_SUB):
        rows = slice(r0, r0 + INPROJ_SUB)
        xn = _rms(h_ref[rows, :], g_ref[...]).astype(BF16)

        def seg(lo, n):
            return _dot(xn, w_ref[:, lo:lo + n])

        sb_ref[rows, :] = seg(0, cw).astype(BF16)
        t_ref[rows, :] = (seg(cw, cw) * seg(2 * cw, cw)).astype(BF16)
        q_ref[rows, :] = seg(base, MLSTM_WIDTH).astype(BF16)
        k_ref[rows, :] = seg(base + MLSTM_WIDTH, MLSTM_WIDTH).astype(BF16)
        v_ref[rows, :] = seg(base + 2 * MLSTM_WIDTH, MLSTM_WIDTH).astype(BF16)
        og_ref[rows, :] = seg(base + 3 * MLSTM_WIDTH, MLSTM_WIDTH).astype(BF16)
        gt_ref[rows, :] = seg(base + 4 * MLSTM_WIDTH, GATE_PAD)[:, :ODD_GATES] + gb_ref[...]


def _inproj_odd(h2d, gain, w_pad, gate_bias, tm):
    t, d = h2d.shape
    row = lambda i: (i, 0)
    const = lambda i: (0, 0)
    widths = (CONV_WIDTH, CONV_WIDTH) + (MLSTM_WIDTH,) * 4 + (ODD_GATES,)
    dts = (BF16,) * 6 + (F32,)
    return pl.pallas_call(
        _inproj_odd_kernel,
        grid=(t // tm,),
        in_specs=[pl.BlockSpec((tm, d), row), pl.BlockSpec((1, d), const),
                  pl.BlockSpec((d, ODD_W), const), pl.BlockSpec((1, ODD_GATES), const)],
        out_specs=[pl.BlockSpec((tm, w), row) for w in widths],
        out_shape=[jax.ShapeDtypeStruct((t, w), dt) for w, dt in zip(widths, dts)],
        compiler_params=_cparams("parallel"),
        name="inproj_odd",
    )(h2d, gain, w_pad, gate_bias)


def _fnet_factors(s):
    lg = int(round(math.log2(s)))
    assert 2 ** lg == s
    n1 = 2 ** ((lg + 1) // 2)
    return n1, s // n1


@functools.lru_cache(maxsize=None)
def _fnet_tables(s):
    n1, n2 = _fnet_factors(s)
    k1 = np.arange(n1, dtype=np.int64)[None, :, None]
    m1 = np.arange(n1, dtype=np.int64)[None, None, :]
    j2 = np.arange(n2, dtype=np.int64)[:, None, None]
    ang = 2.0 * np.pi * ((k1 * (j2 + n2 * m1)) % s).astype(np.float64) / s
    gc = np.cos(ang) / np.sqrt(n1)
    gs = np.sin(ang) / np.sqrt(n1)
    g = np.concatenate([np.concatenate([gc, -gs], axis=2), np.concatenate([-gs, -gc], axis=2)], axis=1)
    k2 = np.arange(n2, dtype=np.int64)[:, None]
    m2 = np.arange(n2, dtype=np.int64)[None, :]
    ang2 = 2.0 * np.pi * ((k2 * m2) % n2).astype(np.float64) / n2
    f2 = np.concatenate([np.cos(ang2), np.sin(ang2)], axis=1) / np.sqrt(n2)
    return g.astype(np.float32).astype(BF16), f2.astype(np.float32).astype(BF16)


@functools.lru_cache(maxsize=None)
def _channel_dft():
    j = np.arange(FNET_WIDTH)
    same = (j[:, None] // FNET_GROUP_DIM) == (j[None, :] // FNET_GROUP_DIM)
    ang = 2.0 * np.pi * (((j[:, None] % FNET_GROUP_DIM) * (j[None, :] % FNET_GROUP_DIM)) % FNET_GROUP_DIM) / FNET_GROUP_DIM
    c = np.where(same, np.cos(ang), 0.0) / np.sqrt(FNET_GROUP_DIM)
    sn = np.where(same, np.sin(ang), 0.0) / np.sqrt(FNET_GROUP_DIM)
    return np.concatenate([c, sn], axis=1).astype(np.float32).astype(BF16)


def _fnet1_kernel(a_ref, b_ref, g_ref, p_ref, q_ref, *, n1, tn2, c):
    for t in range(tn2):
        sl = slice(t * c, (t + 1) * c)
        rhs = jnp.concatenate([a_ref[:, sl], b_ref[:, sl]], axis=0)
        res = _dot(g_ref[t], rhs)
        p_ref[:, sl] = res[:n1].astype(BF16)
        q_ref[:, sl] = res[n1:].astype(BF16)


def _fnet2_kernel(p_ref, q_ref, f_ref, y_ref, *, n2, tk, c):
    for t in range(tk):
        rs = slice(t * n2, (t + 1) * n2)
        rhs = jnp.concatenate([p_ref[rs, :], q_ref[rs, :]], axis=0)
        y_ref[:, t * c:(t + 1) * c] = _dot(f_ref[...], rhs).astype(BF16)


def _fnet(a, b):
    bsz, s, c = a.shape
    n1, n2 = _fnet_factors(s)
    g_tab, f2_tab = _fnet_tables(s)
    tn2 = min(8, n2)
    tk = min(8, n1)
    av = a.reshape(bsz, n1, n2 * c)
    bv = b.reshape(bsz, n1, n2 * c)
    blk = pl.BlockSpec((None, n1, tn2 * c), lambda j, i: (i, 0, j))
    p, q = pl.pallas_call(
        functools.partial(_fnet1_kernel, n1=n1, tn2=tn2, c=c),
        grid=(n2 // tn2, bsz),
        in_specs=[blk, blk, pl.BlockSpec((tn2, 2 * n1, 2 * n1), lambda j, i: (j, 0, 0))],
        out_specs=[blk, blk],
        out_shape=[jax.ShapeDtypeStruct((bsz, n1, n2 * c), BF16)] * 2,
        compiler_params=_cparams("parallel", "parallel"),
        name="fnet_stage1",
    )(av, bv, g_tab)
    pv = p.reshape(bsz, n1 * n2, c)
    qv = q.reshape(bsz, n1 * n2, c)
    rblk = pl.BlockSpec((None, tk * n2, c), lambda i, j: (i, j, 0))
    y = pl.pallas_call(
        functools.partial(_fnet2_kernel, n2=n2, tk=tk, c=c),
        grid=(bsz, n1 // tk),
        in_specs=[rblk, rblk, pl.BlockSpec((n2, 2 * n2), lambda i, j: (0, 0))],
        out_specs=pl.BlockSpec((None, n2, tk * c), lambda i, j: (i, 0, j)),
        out_shape=jax.ShapeDtypeStruct((bsz, n2, n1 * c), BF16),
        compiler_params=_cparams("parallel", "parallel"),
        name="fnet_stage2",
    )(pv, qv, f2_tab)
    return y.reshape(bsz, s, c)


GCHUNK = 128
GHALF = GCHUNK // 2


@functools.lru_cache(maxsize=None)
def _gla_cum_tables(reverse):
    n = GCHUNK
    i = np.arange(n)[:, None]
    t = np.arange(n)[None, :]
    if reverse:
        tri = (t >= i)
        mid = (t >= GHALF)
        rest = (t < i)
    else:
        tri = (t <= i)
        mid = (t <= GHALF - 1)
        rest = (t > i)
    tri_n = tri.astype(np.float32)
    key = np.concatenate([(mid.astype(np.float32) - tri).T, rest.astype(np.float32).T,
                          np.ones((n, n), np.float32)], axis=1)
    return tri_n.astype(BF16), key.astype(BF16)


def _dot_split2(a, b, split_lhs):
    if split_lhs:
        p = _split_bf16(a, 2)
        return _dot(p[0], b) + _dot(p[1], b)
    p = _split_bf16(b, 2)
    return _dot(a, p[0]) + _dot(a, p[1])


def _gla_scan(q_ref, kt_ref, v_ref, g_ref, gt_ref, w2_ref, w2t_ref, b_ref, bt_ref, tri_ref, key_ref,
              o_ref, state_ref, reverse):
    ts = q_ref.shape[0]
    n = GCHUNK
    nck = ts // n
    dk, dv = GLA_DK_PAD, GLA_DV_PAD
    r = lax.broadcasted_iota(jnp.int32, (n, n), 0)
    c = lax.broadcasted_iota(jnp.int32, (n, n), 1)
    mask = (c >= r) if reverse else (c <= r)
    w2 = w2_ref[...].astype(BF16)
    w2t = w2t_ref[...].astype(BF16)
    tri = tri_ref[...]
    key = key_ref[...]
    mid_row = GHALF if reverse else GHALF - 1

    @pl.when(pl.program_id(1) == 0)
    def _():
        state_ref[...] = jnp.zeros_like(state_ref)

    for ci in range(nck):
        ch = nck - 1 - ci if reverse else ci
        rows = slice(ch * n, (ch + 1) * n)
        la = jax.nn.log_sigmoid(_dot(g_ref[rows, :].astype(BF16), w2) + b_ref[...]) * (1.0 / GLA_TAU)
        cum = _dot_split2(tri, la, False)
        qf = q_ref[rows, :].astype(F32)
        q_mid = (qf * jnp.exp(cum - cum[mid_row:mid_row + 1, :])).astype(BF16)
        q_in = (qf * jnp.exp(cum)).astype(BF16)
        lat = jax.nn.log_sigmoid(_dot(w2t, gt_ref[:, rows].astype(BF16)) + bt_ref[...]) * (1.0 / GLA_TAU)
        rel = _dot_split2(lat, key, True)
        ktf = kt_ref[:, rows].astype(F32)
        k_mid = (ktf * jnp.exp(rel[:, :n])).astype(BF16)
        k_out = (ktf * jnp.exp(rel[:, n:2 * n])).astype(BF16)
        dec = jnp.exp(rel[:, 2 * n:])
        for h in range(GLA_HEADS):
            ks = slice(h * dk, (h + 1) * dk)
            vs = slice(h * dv, (h + 1) * dv)
            vh = v_ref[rows, vs]
            att = jnp.where(mask, _dot(q_mid[:, ks], k_mid[ks, :]), 0.0).astype(BF16)
            st = state_ref[h]
            o_ref[rows, vs] = _dot(jnp.concatenate([att, q_in[:, ks]], axis=1),
                                   jnp.concatenate([vh, st.astype(BF16)], axis=0))
            dech = dec[ks, :]
            state_ref[h] = st * jnp.concatenate([dech] * (dv // n), axis=1) + _dot(k_out[ks, :], vh)


def _gla_fwd_kernel(q_ref, kt_ref, v_ref, g_ref, gt_ref, w2_ref, w2t_ref, b_ref, bt_ref, tri_ref, key_ref,
                    of_ref, state_ref, o_scr):
    _gla_scan(q_ref, kt_ref, v_ref, g_ref, gt_ref, w2_ref, w2t_ref, b_ref, bt_ref, tri_ref, key_ref,
              o_scr, state_ref, False)
    of_ref[...] = o_scr[...].astype(BF16)


def _gla_bwd_kernel(q_ref, kt_ref, v_ref, g_ref, gt_ref, w2_ref, w2t_ref, b_ref, bt_ref, tri_ref, key_ref,
                    of_ref, r_ref, yf_ref, h_ref, gn_ref, wf_ref, wg_ref, out_ref, state_ref, o_scr):
    _gla_scan(q_ref, kt_ref, v_ref, g_ref, gt_ref, w2_ref, w2t_ref, b_ref, bt_ref, tri_ref, key_ref,
              o_scr, state_ref, True)
    o = o_scr[...] + of_ref[...].astype(F32)
    normed = []
    for h in range(GLA_HEADS):
        oh = o[:, h * GLA_DV_PAD:(h + 1) * GLA_DV_PAD]
        ms = jnp.sum(oh * oh, axis=-1, keepdims=True) * (1.0 / GLA_DV)
        normed.append(oh * lax.rsqrt(ms + EPS))
    yg = jnp.concatenate(normed, axis=1) * gn_ref[...]
    yg = (yg * jax.nn.silu(r_ref[...].astype(F32))).astype(BF16)
    out_ref[...] = h_ref[...] + _dot(yf_ref[...], wf_ref[...]) + _dot(yg, wg_ref[...])


def _gla(q, kt, v, g, gt, w2f, bf, w2b, bb, r, yf, h, gnorm, w_out_f, w_out_g, ts):
    bsz, s, _ = q.shape
    d = h.shape[-1]
    nblk = s // ts
    gw = 2 * GLA_GATE_RANK
    fwd = lambda b, i: (b, i, 0)
    bwd = lambda b, i: (b, nblk - 1 - i, 0)
    const = lambda b, i: (0, 0)

    def seq_specs(rev):
        blk = (lambda i: nblk - 1 - i) if rev else (lambda i: i)
        im = lambda b, i: (b, blk(i), 0)
        imt = lambda b, i: (b, 0, blk(i))
        return [pl.BlockSpec((None, ts, GLA_QK_PAD), im), pl.BlockSpec((None, GLA_QK_PAD, ts), imt),
                pl.BlockSpec((None, ts, GLA_V_PAD), im), pl.BlockSpec((None, ts, gw), im),
                pl.BlockSpec((None, gw, ts), imt),
                pl.BlockSpec((gw, GLA_QK_PAD), const), pl.BlockSpec((GLA_QK_PAD, gw), const),
                pl.BlockSpec((1, GLA_QK_PAD), const), pl.BlockSpec((GLA_QK_PAD, GCHUNK), const),
                pl.BlockSpec((GCHUNK, GCHUNK), const), pl.BlockSpec((GCHUNK, 3 * GCHUNK), const)]

    def gate_args(w2, bias, rev):
        tri, key = _gla_cum_tables(rev)
        return (w2, w2.T, bias, jnp.broadcast_to(bias.T, (GLA_QK_PAD, GCHUNK)), tri, key)

    scratch = [pltpu.VMEM((GLA_HEADS, GLA_DK_PAD, GLA_DV_PAD), F32), pltpu.VMEM((ts, GLA_V_PAD), F32)]
    o_f = pl.pallas_call(
        _gla_fwd_kernel,
        grid=(bsz, nblk),
        in_specs=seq_specs(False),
        out_specs=pl.BlockSpec((None, ts, GLA_V_PAD), fwd),
        out_shape=jax.ShapeDtypeStruct((bsz, s, GLA_V_PAD), BF16),
        scratch_shapes=scratch,
        compiler_params=_cparams("parallel", "arbitrary"),
        name="gla_fwd",
    )(q, kt, v, g, gt, *gate_args(w2f, bf, False))
    return pl.pallas_call(
        _gla_bwd_kernel,
        grid=(bsz, nblk),
        in_specs=seq_specs(True) + [
            pl.BlockSpec((None, ts, GLA_V_PAD), bwd), pl.BlockSpec((None, ts, GLA_V_PAD), bwd),
            pl.BlockSpec((None, ts, FNET_WIDTH), bwd), pl.BlockSpec((None, ts, d), bwd),
            pl.BlockSpec((1, GLA_V_PAD), const), pl.BlockSpec((FNET_WIDTH, d), const),
            pl.BlockSpec((GLA_V_PAD, d), const)],
        out_specs=pl.BlockSpec((None, ts, d), bwd),
        out_shape=jax.ShapeDtypeStruct((bsz, s, d), F32),
        scratch_shapes=scratch,
        compiler_params=_cparams("parallel", "arbitrary"),
        name="gla_bwd_out",
    )(q, kt, v, g, gt, *gate_args(w2b, bb, True), o_f, r, yf, h, gnorm, w_out_f, w_out_g)


MCHUNK = 128
assert MCHUNK == MLSTM_DH
MGATE_ROWS = 2 * MLSTM_HEADS
ZROWS = 80
ZCOLS = 3 * MLSTM_DH


@functools.lru_cache(maxsize=None)
def _mlstm_zsel(reverse):
    z = np.zeros((ZROWS, MLSTM_HEADS * ZCOLS), np.float32)
    rbase = MLSTM_HEADS if reverse else 0
    for h in range(MLSTM_HEADS):
        for t in range(3):
            for k in range(3):
                z[(3 * t + k) * MGATE_ROWS + rbase + h, h * ZCOLS + t * MLSTM_DH:h * ZCOLS + (t + 1) * MLSTM_DH] = 1.0
    return z.astype(BF16)


def _cummax_lanes(x, reverse):
    n = x.shape[-1]
    lane = lax.broadcasted_iota(jnp.int32, x.shape, 1)
    s = 1
    while s < n:
        if reverse:
            shifted = jnp.where(lane < n - s, pltpu.roll(x, n - s, axis=1), -jnp.inf)
        else:
            shifted = jnp.where(lane >= s, pltpu.roll(x, s, axis=1), -jnp.inf)
        x = jnp.maximum(x, shifted)
        s *= 2
    return x


def _mlstm_scan(q_ref, kt_ref, v_ref, gi_ref, gf_ref, zsel_ref, o_ref, st_ref, m_ref, reverse):
    ts = q_ref.shape[0]
    n = MCHUNK
    nck = ts // n
    dh = MLSTM_DH
    gr = MGATE_ROWS
    mask, tri_t_bf = _tri_masks(n, reverse)
    rbase = MLSTM_HEADS if reverse else 0

    @pl.when(pl.program_id(1) == 0)
    def _():
        st_ref[...] = jnp.zeros_like(st_ref)
        m_ref[...] = jnp.zeros_like(m_ref)

    gi = gi_ref[...].reshape(nck * gr, n)
    gf = gf_ref[...].reshape(nck * gr, n)
    cum = _cumsum_cols(jax.nn.log_sigmoid(gf), tri_t_bf)
    b = gi - cum
    md = cum + _cummax_lanes(b, reverse)
    tot = cum[:, 0:1] if reverse else cum[:, n - 1:n]
    w_end = tot + b
    m_loc = jnp.max(w_end, axis=1, keepdims=True)
    e = jnp.exp(w_end - m_loc)
    ones = jnp.ones((n, dh), BF16)
    zpad = jnp.zeros((gr, n), BF16)
    zsel = zsel_ref[...]

    for ci in range(nck):
        c = nck - 1 - ci if reverse else ci
        rs = slice(c * gr, (c + 1) * gr)
        rows = slice(c * n, (c + 1) * n)
        m = m_ref[...]
        lw = cum[rs] + m[:, :n]
        m_t = jnp.maximum(lw, md[rs])
        pieces = _split_bf16(cum[rs] - m_t, 3) + _split_bf16(lw - m_t, 3) + _split_bf16(-m_t, 3) + [zpad]
        z = _dot_tn(jnp.concatenate(pieces, axis=0), zsel)
        m_new = jnp.maximum(tot[rs] + m, m_loc[rs])
        fa = jnp.exp(tot[rs] + m - m_new)
        fb = jnp.exp(m_loc[rs] - m_new)
        m_ref[...] = m_new
        for h in range(MLSTM_HEADS):
            r = c * gr + rbase + h
            hs = slice(h * dh, (h + 1) * dh)
            zb = h * ZCOLS
            qh = q_ref[rows, hs]
            kth = kt_ref[hs, rows]
            vaug = jnp.concatenate([v_ref[rows, hs], ones], axis=1)
            expo = jnp.where(mask, z[:, zb:zb + dh] + b[r:r + 1, :], -jnp.inf)
            s_qk = (_dot(qh, kth) * jnp.exp(expo)).astype(BF16)
            qa = (qh.astype(F32) * jnp.exp(z[:, zb + dh:zb + 2 * dh])).astype(BF16)
            st = st_ref[h]
            res = _dot(jnp.concatenate([s_qk, qa], axis=1), jnp.concatenate([vaug, st.astype(BF16)], axis=0))
            floor = jnp.exp(z[:, zb + 2 * dh:zb + 3 * dh])
            o_ref[rows, hs] = res[:, :dh] / jnp.maximum(jnp.abs(res[:, dh:]), floor)
            ket = (kth.astype(F32) * e[r:r + 1, :]).astype(BF16)
            rr = rbase + h
            st_ref[h] = fa[rr:rr + 1, :] * st + fb[rr:rr + 1, :] * _dot(ket, vaug)


def _mlstm_fwd_kernel(q_ref, kt_ref, v_ref, gi_ref, gf_ref, zsel_ref, hf_ref, st_ref, m_ref, o_scr):
    _mlstm_scan(q_ref, kt_ref, v_ref, gi_ref, gf_ref, zsel_ref, o_scr, st_ref, m_ref, False)
    hf_ref[...] = o_scr[...].astype(BF16)


def _mlstm_bwd_kernel(q_ref, kt_ref, v_ref, gi_ref, gf_ref, zsel_ref, hf_ref, og_ref, sb_ref, t_ref, tp_ref, tn_ref,
                      h_ref, mn_ref, cw_ref, cb_ref, wc_ref, wm_ref, out_ref, st_ref, m_ref, o_scr):
    _mlstm_scan(q_ref, kt_ref, v_ref, gi_ref, gf_ref, zsel_ref, o_scr, st_ref, m_ref, True)
    ts = q_ref.shape[0]
    o = o_scr[...] + hf_ref[...].astype(F32)
    normed = []
    for h in range(MLSTM_HEADS):
        oh = o[:, h * MLSTM_DH:(h + 1) * MLSTM_DH]
        normed.append(oh * lax.rsqrt(jnp.mean(oh * oh, axis=-1, keepdims=True) + EPS))
    ym = jnp.concatenate(normed, axis=1) * mn_ref[...]
    ym = (ym * jax.nn.sigmoid(og_ref[...].astype(F32))).astype(BF16)
    i = pl.program_id(1)
    nblk = pl.num_programs(1)
    prev = jnp.where(i < nblk - 1, tp_ref[...].astype(F32), 0.0)
    nxt = jnp.where(i > 0, tn_ref[...].astype(F32), 0.0)
    ext = jnp.concatenate([prev, t_ref[...].astype(F32), nxt], axis=0)
    rows = ts + 2 * HALO_BF16
    lo = pltpu.roll(ext, 1, axis=0)[HALO_BF16:HALO_BF16 + ts]
    hi = pltpu.roll(ext, rows - 1, axis=0)[HALO_BF16:HALO_BF16 + ts]
    cw = cw_ref[...]
    conv = lo * cw[0:1, :] + ext[HALO_BF16:HALO_BF16 + ts] * cw[1:2, :] + hi * cw[2:3, :] + cb_ref[...]
    yc = (sb_ref[...].astype(F32) * conv).astype(BF16)
    out_ref[...] = h_ref[...] + _dot(yc, wc_ref[...]) + _dot(ym, wm_ref[...])


def _mlstm(q, kt, v, gi, gf, og, sb, t, h, mnorm, conv_w, conv_b, w_out_c, w_out_m, ts):
    bsz, s, _ = q.shape
    d = h.shape[-1]
    nblk = s // ts
    nck = ts // MCHUNK
    hb = ts // HALO_BF16
    fwd = lambda b, i: (b, i, 0)
    bwd = lambda b, i: (b, nblk - 1 - i, 0)
    const = lambda b, i: (0, 0)

    def seq_specs(rev):
        w = MLSTM_WIDTH
        blk = (lambda i: nblk - 1 - i) if rev else (lambda i: i)
        im = lambda b, i: (b, blk(i), 0)
        imt = lambda b, i: (b, 0, blk(i))
        im4 = lambda b, i: (b, blk(i), 0, 0)
        gspec = pl.BlockSpec((None, nck, MGATE_ROWS, MCHUNK), im4)
        return [pl.BlockSpec((None, ts, w), im), pl.BlockSpec((None, w, ts), imt), pl.BlockSpec((None, ts, w), im),
                gspec, gspec, pl.BlockSpec((ZROWS, MLSTM_HEADS * ZCOLS), const)]

    scratch = [pltpu.VMEM((MLSTM_HEADS, MLSTM_DH, 2 * MLSTM_DH), F32), pltpu.VMEM((MGATE_ROWS, 2 * MLSTM_DH), F32),
               pltpu.VMEM((ts, MLSTM_WIDTH), F32)]
    h_f = pl.pallas_call(
        _mlstm_fwd_kernel,
        grid=(bsz, nblk),
        in_specs=seq_specs(False),
        out_specs=pl.BlockSpec((None, ts, MLSTM_WIDTH), fwd),
        out_shape=jax.ShapeDtypeStruct((bsz, s, MLSTM_WIDTH), BF16),
        scratch_shapes=scratch,
        compiler_params=_cparams("parallel", "arbitrary"),
        name="mlstm_fwd",
    )(q, kt, v, gi, gf, _mlstm_zsel(False))
    last16 = s // HALO_BF16 - 1
    halo_prev = lambda b, i: (b, jnp.maximum((nblk - 1 - i) * hb - 1, 0), 0)
    halo_next = lambda b, i: (b, jnp.minimum((nblk - i) * hb, last16), 0)
    return pl.pallas_call(
        _mlstm_bwd_kernel,
        grid=(bsz, nblk),
        in_specs=seq_specs(True) + [
            pl.BlockSpec((None, ts, MLSTM_WIDTH), bwd), pl.BlockSpec((None, ts, MLSTM_WIDTH), bwd),
            pl.BlockSpec((None, ts, CONV_WIDTH), bwd), pl.BlockSpec((None, ts, CONV_WIDTH), bwd),
            pl.BlockSpec((None, HALO_BF16, CONV_WIDTH), halo_prev),
            pl.BlockSpec((None, HALO_BF16, CONV_WIDTH), halo_next),
            pl.BlockSpec((None, ts, d), bwd),
            pl.BlockSpec((1, MLSTM_WIDTH), const), pl.BlockSpec((3, CONV_WIDTH), const),
            pl.BlockSpec((1, CONV_WIDTH), const), pl.BlockSpec((CONV_WIDTH, d), const),
            pl.BlockSpec((MLSTM_WIDTH, d), const)],
        out_specs=pl.BlockSpec((None, ts, d), bwd),
        out_shape=jax.ShapeDtypeStruct((bsz, s, d), F32),
        scratch_shapes=scratch,
        compiler_params=_cparams("parallel", "arbitrary"),
        name="mlstm_bwd_out",
    )(q, kt, v, gi, gf, _mlstm_zsel(True), h_f, og, sb, t, t, t, h, mnorm, conv_w, conv_b, w_out_c, w_out_m)


def _ffn_kernel(h_ref, hp_ref, hn_ref, p_ref, fg_ref, wu_ref, cw_ref, cb_ref, wd_ref, pg_ref, pgw_ref, pw_ref,
                fn_ref, out_ref, xn_scr, act_scr, *, d_ff, f_tile, final):
    ts = h_ref.shape[0]
    i = pl.program_id(1)
    nblk = pl.num_programs(1)
    x = h_ref[...]
    fg = fg_ref[...]
    xn_scr[0:HALO, :] = jnp.where(i > 0, _rms(hp_ref[...], fg), 0.0).astype(BF16)
    xn_scr[HALO:HALO + ts, :] = _rms(x, fg).astype(BF16)
    xn_scr[HALO + ts:, :] = jnp.where(i < nblk - 1, _rms(hn_ref[...], fg), 0.0).astype(BF16)
    rows = ts + 2 * HALO
    xe = xn_scr[...]
    xc = xe[HALO:HALO + ts]
    for f0 in range(0, d_ff, f_tile):
        gate = _dot(xe, wu_ref[:, f0:f0 + f_tile])
        val = _dot(xc, wu_ref[:, d_ff + f0:d_ff + f0 + f_tile])
        cw = cw_ref[:, f0:f0 + f_tile]
        lo = pltpu.roll(gate, 1, axis=0)[HALO:HALO + ts]
        hi = pltpu.roll(gate, rows - 1, axis=0)[HALO:HALO + ts]
        conv = (lo * cw[0:1, :] + gate[HALO:HALO + ts] * cw[1:2, :] + hi * cw[2:3, :]
                + cb_ref[:, f0:f0 + f_tile])
        act_scr[:, f0:f0 + f_tile] = (jax.nn.silu(conv) * val).astype(BF16)
    h2 = x + _dot(act_scr[...], wd_ref[...])
    gate = jax.nn.sigmoid(_dot(_rms(h2, pg_ref[...]).astype(BF16), pgw_ref[...]))
    h3 = h2 + _dot(p_ref[...].astype(BF16), pw_ref[...]) * gate
    out_ref[...] = _rms(h3, fn_ref[...]) if final else h3


def _ffn(h, p, layer, fgain, w_up, conv_w, conv_b, w_down, pgain, pg_w, p_w, final_gain, final, ts, f_tile):
    bsz, s, d = h.shape
    d_ff = w_down.shape[0]
    nblk = s // ts
    hb = ts // HALO
    last8 = s // HALO - 1
    blk = lambda b, i: (b, i, 0)
    const = lambda b, i: (0, 0)
    once = pl.Buffered(1)

    def wspec(shape):
        return pl.BlockSpec(shape, const, pipeline_mode=once)

    return pl.pallas_call(
        functools.partial(_ffn_kernel, d_ff=d_ff, f_tile=f_tile, final=final),
        grid=(bsz, nblk),
        in_specs=[pl.BlockSpec((None, ts, d), blk),
                  pl.BlockSpec((None, HALO, d), lambda b, i: (b, jnp.maximum(i * hb - 1, 0), 0)),
                  pl.BlockSpec((None, HALO, d), lambda b, i: (b, jnp.minimum((i + 1) * hb, last8), 0)),
                  pl.BlockSpec((None, None, ts, p.shape[-1]), lambda b, i: (layer, b, i, 0)),
                  wspec((1, d)), wspec((d, 2 * d_ff)), wspec((3, d_ff)), wspec((1, d_ff)), wspec((d_ff, d)),
                  wspec((1, d)), wspec((d, d)), wspec((p.shape[-1], d)), wspec((1, d))],
        out_specs=pl.BlockSpec((None, ts, d), blk),
        out_shape=jax.ShapeDtypeStruct((bsz, s, d), F32),
        scratch_shapes=[pltpu.VMEM((ts + 2 * HALO, d), BF16), pltpu.VMEM((ts, d_ff), BF16)],
        compiler_params=_cparams("parallel", "parallel"),
        name="ffn_ple",
    )(h, h, h, p, fgain, w_up, conv_w, conv_b, w_down, pgain, pg_w, p_w, final_gain)


def _pad_heads_cols(w, heads, dh, dpad):
    lead = w.shape[:-1]
    w = w.reshape(lead + (heads, dh))
    w = jnp.pad(w, [(0, 0)] * len(lead) + [(0, 0), (0, dpad - dh)])
    return w.reshape(lead + (heads * dpad,))


def _pad_cols(w, width):
    return jnp.pad(w, [(0, 0)] * (w.ndim - 1) + [(0, width - w.shape[-1])])


def _even_weights(w_in, w2_f, b_f, w2_b, b_b, gnorm, w_out):
    cuts = np.cumsum([FNET_WIDTH, GLA_HEADS * GLA_DK, GLA_HEADS * GLA_DK, GLA_HEADS * GLA_DV,
                      GLA_HEADS * GLA_DV]).tolist()
    wu, wq, wk, wv, wr, wg = jnp.split(w_in, cuts, axis=-1)
    w_pad = jnp.concatenate([
        wu, _pad_heads_cols(wq * (GLA_DK ** -0.5), GLA_HEADS, GLA_DK, GLA_DK_PAD),
        _pad_heads_cols(wk, GLA_HEADS, GLA_DK, GLA_DK_PAD),
        _pad_heads_cols(wv, GLA_HEADS, GLA_DV, GLA_DV_PAD), _pad_heads_cols(wr, GLA_HEADS, GLA_DV, GLA_DV_PAD),
        _pad_cols(wg, GATE_PAD)], axis=-1).astype(BF16)
    zero = jnp.zeros((GLA_GATE_RANK, GLA_QK_PAD), F32)
    w2f = jnp.concatenate([_pad_heads_cols(w2_f, GLA_HEADS, GLA_DK, GLA_DK_PAD), zero], axis=0)
    w2b = jnp.concatenate([zero, _pad_heads_cols(w2_b, GLA_HEADS, GLA_DK, GLA_DK_PAD)], axis=0)
    bf = _pad_heads_cols(b_f[None, :], GLA_HEADS, GLA_DK, GLA_DK_PAD)
    bb = _pad_heads_cols(b_b[None, :], GLA_HEADS, GLA_DK, GLA_DK_PAD)
    gn = _pad_heads_cols(gnorm[None, :], GLA_HEADS, GLA_DV, GLA_DV_PAD)
    w_out_f = w_out[:FNET_WIDTH].astype(BF16)
    wog = w_out[FNET_WIDTH:].reshape(GLA_HEADS, GLA_DV, -1)
    wog = jnp.pad(wog, ((0, 0), (0, GLA_DV_PAD - GLA_DV), (0, 0))).reshape(GLA_V_PAD, -1).astype(BF16)
    return w_pad, w2f, bf, w2b, bb, gn, w_out_f, wog


def _trunk(x, p, e_norm, e_w_in, e_gla_w2_f, e_gla_b_f, e_gla_w2_b, e_gla_b_b, e_gla_norm, e_w_out,
           o_norm, o_w_in, o_conv_w, o_conv_b, o_gate_bias, o_mlstm_norm, o_w_out,
           ffn_norm, ffn_w_up, ffn_conv_w, ffn_conv_b, ffn_w_down,
           ple_w, ple_gate_norm, ple_gate_w, final_norm, *, ts=ROW_TILE, f_tile=256):
    bsz, s, d = x.shape
    depth = p.shape[0]
    t = bsz * s
    ts = min(ts, s)
    h = x
    for layer in range(depth):
        j = layer // 2
        h2d = h.reshape(t, d)
        if layer % 2 == 0:
            w_pad, w2f, bf, w2b, bb, gn, w_out_f, w_out_g = _even_weights(
                e_w_in[j], e_gla_w2_f[j], e_gla_b_f[j], e_gla_w2_b[j], e_gla_b_b[j], e_gla_norm[j], e_w_out[j])
            a, b, q, k, v, r, g = _inproj_even(h2d, e_norm[j][None, :], w_pad, _channel_dft(), min(2 * ts, s))
            sh = lambda z: z.reshape(bsz, s, z.shape[-1])
            yf = _fnet(sh(a), sh(b))
            g = sh(g)
            h = _gla(sh(q), sh(k).transpose(0, 2, 1), sh(v), g, g.transpose(0, 2, 1), w2f, bf, w2b, bb,
                     sh(r), yf, h, gn, w_out_f, w_out_g, ts)
        else:
            q_lo = 3 * CONV_WIDTH
            col = jnp.arange(o_w_in.shape[-1])
            qscale = jnp.where((col >= q_lo) & (col < q_lo + MLSTM_WIDTH), MLSTM_DH ** -0.5, 1.0)
            w_pad = _pad_cols(o_w_in[j] * qscale, ODD_W).astype(BF16)
            sb, tt, q, k, v, og, gates = _inproj_odd(h2d, o_norm[j][None, :], w_pad, o_gate_bias[j][None, :], min(2 * ts, s))
            sh = lambda z: z.reshape(bsz, s, z.shape[-1])
            nh = MLSTM_HEADS
            gates = gates.reshape(bsz, s // MCHUNK, MCHUNK, ODD_GATES).transpose(0, 1, 3, 2)
            gi = jnp.concatenate([gates[:, :, 0:nh], gates[:, :, 2 * nh:3 * nh]], axis=2)
            gf = jnp.concatenate([gates[:, :, nh:2 * nh], gates[:, :, 3 * nh:4 * nh]], axis=2)
            w_out = o_w_out[j].astype(BF16)
            h = _mlstm(sh(q), sh(k).transpose(0, 2, 1), sh(v), gi, gf, sh(og), sh(sb), sh(tt), h,
                       o_mlstm_norm[j][None, :], o_conv_w[j], o_conv_b[j][None, :],
                       w_out[:CONV_WIDTH], w_out[CONV_WIDTH:], ts)
        h = _ffn(h, p, layer, ffn_norm[layer][None, :], ffn_w_up[layer].astype(BF16), ffn_conv_w[layer],
                 ffn_conv_b[layer][None, :], ffn_w_down[layer].astype(BF16), ple_gate_norm[layer][None, :],
                 ple_gate_w[layer].astype(BF16), ple_w[layer].astype(BF16), final_norm[None, :],
                 layer == depth - 1, min(2 * ts, s), f_tile)
    return h


def kernel(x_prompt, x_sample, p_prompt, p_sample, e_norm, e_w_in, e_gla_w2_f, e_gla_b_f, e_gla_w2_b, e_gla_b_b, e_gla_norm, e_w_out, o_norm, o_w_in, o_conv_w, o_conv_b, o_gate_bias, o_mlstm_norm, o_w_out, ffn_norm, ffn_w_up, ffn_conv_w, ffn_conv_b, ffn_w_down, ple_w, ple_gate_norm, ple_gate_w, final_norm):
    weights = (e_norm, e_w_in, e_gla_w2_f, e_gla_b_f, e_gla_w2_b, e_gla_b_b, e_gla_norm, e_w_out,
               o_norm, o_w_in, o_conv_w, o_conv_b, o_gate_bias, o_mlstm_norm, o_w_out,
               ffn_norm, ffn_w_up, ffn_conv_w, ffn_conv_b, ffn_w_down,
               ple_w, ple_gate_norm, ple_gate_w, final_norm)
    return (_trunk(x_prompt, p_prompt, *weights), _trunk(x_sample, p_sample, *weights))
```

```python
import functools
import math

import numpy as np
import jax
import jax.numpy as jnp
from jax import lax
from jax.experimental import pallas as pl
from jax.experimental.pallas import tpu as pltpu

F32 = jnp.float32
BF16 = jnp.bfloat16
EPS = 1e-6

FNET_GROUPS = 4
FNET_GROUP_DIM = 64
FNET_WIDTH = FNET_GROUPS * FNET_GROUP_DIM
GLA_HEADS = 4
GLA_DK = 96
GLA_DV = 192
GLA_GATE_RANK = 16
GLA_TAU = 16.0
CONV_WIDTH = 512
MLSTM_HEADS = 4
MLSTM_DH = 128
MLSTM_WIDTH = MLSTM_HEADS * MLSTM_DH

GLA_DK_PAD = 128
GLA_DV_PAD = 256
GLA_QK_PAD = GLA_HEADS * GLA_DK_PAD
GLA_V_PAD = GLA_HEADS * GLA_DV_PAD
GATE_PAD = 128

V7X_VMEM_BYTES = 64 * 1024 * 1024
VMEM_LIMIT = V7X_VMEM_BYTES - 8 * 1024 * 1024

ROW_TILE = 512
HALO = 8
HALO_BF16 = 16


def _cparams(*sem):
    return pltpu.CompilerParams(dimension_semantics=sem, vmem_limit_bytes=VMEM_LIMIT)


def _rms(x, g):
    ms = jnp.mean(x * x, axis=-1, keepdims=True)
    return x * lax.rsqrt(ms + EPS) * g


def _split_bf16(x, n):
    parts = []
    r = x
    for _ in range(n):
        p = r.astype(BF16)
        parts.append(p)
        r = r - p.astype(F32)
    return parts


def _dot(a, b):
    return jnp.dot(a, b, preferred_element_type=F32)


def _log_sigmoid(x):
    return jnp.minimum(x, 0.0) - jnp.log(1.0 + jnp.exp(-jnp.abs(x)))


def _dot_tn(a, b):
    return lax.dot_general(a, b, (((0,), (0,)), ((), ())), preferred_element_type=F32)


def _tri_masks(n, reverse):
    r = lax.broadcasted_iota(jnp.int32, (n, n), 0)
    c = lax.broadcasted_iota(jnp.int32, (n, n), 1)
    mask = (c >= r) if reverse else (c <= r)
    return mask, ((r >= c) if reverse else (r <= c)).astype(BF16)


def _cumsum_cols(x, tri_t_bf):
    acc = None
    for p in _split_bf16(x, 3):
        t = _dot(p, tri_t_bf)
        acc = t if acc is None else acc + t
    return acc


INPROJ_SUB = 512
EVEN_SEGS = (FNET_WIDTH, GLA_QK_PAD, GLA_QK_PAD, GLA_V_PAD, GLA_V_PAD, GATE_PAD)
EVEN_W = sum(EVEN_SEGS)


def _inproj_even_kernel(h_ref, g_ref, w_ref, dft_ref, a_ref, b_ref, q_ref, k_ref, v_ref, r_ref, gg_ref):
    offs = np.cumsum((0,) + EVEN_SEGS)
    tm = h_ref.shape[0]
    for r0 in range(0, tm, INPROJ_SUB):
        rows = slice(r0, r0 + INPROJ_SUB)
        xn = _rms(h_ref[rows, :], g_ref[...]).astype(BF16)

        def seg(i):
            return _dot(xn, w_ref[:, offs[i]:offs[i + 1]])

        u = seg(0).astype(BF16)
        ab = _dot(u, dft_ref[...])
        a_ref[rows, :] = ab[:, :FNET_WIDTH].astype(BF16)
        b_ref[rows, :] = ab[:, FNET_WIDTH:].astype(BF16)
        q_ref[rows, :] = seg(1).astype(BF16)
        k_ref[rows, :] = seg(2).astype(BF16)
        v_ref[rows, :] = seg(3).astype(BF16)
        r_ref[rows, :] = seg(4).astype(BF16)
        gg_ref[rows, :] = seg(5)[:, :2 * GLA_GATE_RANK]


def _inproj_even(h2d, gain, w_pad, dft_cs, tm):
    t, d = h2d.shape
    row = lambda i: (i, 0)
    const = lambda i: (0, 0)
    widths = (FNET_WIDTH, FNET_WIDTH, GLA_QK_PAD, GLA_QK_PAD, GLA_V_PAD, GLA_V_PAD, 2 * GLA_GATE_RANK)
    dts = (BF16,) * 6 + (F32,)
    return pl.pallas_call(
        _inproj_even_kernel,
        grid=(t // tm,),
        in_specs=[pl.BlockSpec((tm, d), row), pl.BlockSpec((1, d), const),
                  pl.BlockSpec((d, EVEN_W), const), pl.BlockSpec((FNET_WIDTH, 2 * FNET_WIDTH), const)],
        out_specs=[pl.BlockSpec((tm, w), row) for w in widths],
        out_shape=[jax.ShapeDtypeStruct((t, w), dt) for w, dt in zip(widths, dts)],
        compiler_params=_cparams("parallel"),
        name="inproj_even",
    )(h2d, gain, w_pad, dft_cs)


ODD_GATES = 4 * MLSTM_HEADS
ODD_W = 3 * CONV_WIDTH + 4 * MLSTM_WIDTH + GATE_PAD


def _inproj_odd_kernel(h_ref, g_ref, w_ref, gb_ref, sb_ref, t_ref, q_ref, k_ref, v_ref, og_ref, gt_ref):
    cw = CONV_WIDTH
    base = 3 * cw
    tm = h_ref.shape[0]
    for r0 in range(0, tm, INPROJ_SUB):
        rows = slice(r0, r0 + INPROJ_SUB)
        xn = _rms(h_ref[rows, :], g_ref[...]).astype(BF16)

        def seg(lo, n):
            return _dot(xn, w_ref[:, lo:lo + n])

        sb_ref[rows, :] = seg(0, cw).astype(BF16)
        t_ref[rows, :] = (seg(cw, cw) * seg(2 * cw, cw)).astype(BF16)
        q_ref[rows, :] = seg(base, MLSTM_WIDTH).astype(BF16)
        k_ref[rows, :] = seg(base + MLSTM_WIDTH, MLSTM_WIDTH).astype(BF16)
        v_ref[rows, :] = seg(base + 2 * MLSTM_WIDTH, MLSTM_WIDTH).astype(BF16)
        og_ref[rows, :] = seg(base + 3 * MLSTM_WIDTH, MLSTM_WIDTH).astype(BF16)
        gt_ref[rows, :] = seg(base + 4 * MLSTM_WIDTH, GATE_PAD)[:, :ODD_GATES] + gb_ref[...]


def _inproj_odd(h2d, gain, w_pad, gate_bias, tm):
    t, d = h2d.shape
    row = lambda i: (i, 0)
    const = lambda i: (0, 0)
    widths = (CONV_WIDTH, CONV_WIDTH) + (MLSTM_WIDTH,) * 4 + (ODD_GATES,)
    dts = (BF16,) * 6 + (F32,)
    return pl.pallas_call(
        _inproj_odd_kernel,
        grid=(t // tm,),
        in_specs=[pl.BlockSpec((tm, d), row), pl.BlockSpec((1, d), const),
                  pl.BlockSpec((d, ODD_W), const), pl.BlockSpec((1, ODD_GATES), const)],
        out_specs=[pl.BlockSpec((tm, w), row) for w in widths],
        out_shape=[jax.ShapeDtypeStruct((t, w), dt) for w, dt in zip(widths, dts)],
        compiler_params=_cparams("parallel"),
        name="inproj_odd",
    )(h2d, gain, w_pad, gate_bias)


def _fnet_factors(s):
    lg = int(round(math.log2(s)))
    assert 2 ** lg == s
    n1 = 2 ** ((lg + 1) // 2)
    return n1, s // n1


@functools.lru_cache(maxsize=None)
def _fnet_tables(s):
    n1, n2 = _fnet_factors(s)
    k1 = np.arange(n1, dtype=np.int64)[None, :, None]
    m1 = np.arange(n1, dtype=np.int64)[None, None, :]
    j2 = np.arange(n2, dtype=np.int64)[:, None, None]
    ang = 2.0 * np.pi * ((k1 * (j2 + n2 * m1)) % s).astype(np.float64) / s
    gc = np.cos(ang) / np.sqrt(n1)
    gs = np.sin(ang) / np.sqrt(n1)
    g = np.concatenate([np.concatenate([gc, -gs], axis=2), np.concatenate([-gs, -gc], axis=2)], axis=1)
    k2 = np.arange(n2, dtype=np.int64)[:, None]
    m2 = np.arange(n2, dtype=np.int64)[None, :]
    ang2 = 2.0 * np.pi * ((k2 * m2) % n2).astype(np.float64) / n2
    f2 = np.concatenate([np.cos(ang2), np.sin(ang2)], axis=1) / np.sqrt(n2)
    return g.astype(np.float32).astype(BF16), f2.astype(np.float32).astype(BF16)


@functools.lru_cache(maxsize=None)
def _channel_dft():
    j = np.arange(FNET_WIDTH)
    same = (j[:, None] // FNET_GROUP_DIM) == (j[None, :] // FNET_GROUP_DIM)
    ang = 2.0 * np.pi * (((j[:, None] % FNET_GROUP_DIM) * (j[None, :] % FNET_GROUP_DIM)) % FNET_GROUP_DIM) / FNET_GROUP_DIM
    c = np.where(same, np.cos(ang), 0.0) / np.sqrt(FNET_GROUP_DIM)
    sn = np.where(same, np.sin(ang), 0.0) / np.sqrt(FNET_GROUP_DIM)
    return np.concatenate([c, sn], axis=1).astype(np.float32).astype(BF16)


def _fnet1_kernel(a_ref, b_ref, g_ref, p_ref, q_ref, *, n1, tn2, c):
    for t in range(tn2):
        sl = slice(t * c, (t + 1) * c)
        rhs = jnp.concatenate([a_ref[:, sl], b_ref[:, sl]], axis=0)
        res = _dot(g_ref[t], rhs)
        p_ref[:, sl] = res[:n1].astype(BF16)
        q_ref[:, sl] = res[n1:].astype(BF16)


def _fnet2_kernel(p_ref, q_ref, f_ref, y_ref, *, n2, tk, c):
    for t in range(tk):
        rs = slice(t * n2, (t + 1) * n2)
        rhs = jnp.concatenate([p_ref[rs, :], q_ref[rs, :]], axis=0)
        y_ref[:, t * c:(t + 1) * c] = _dot(f_ref[...], rhs).astype(BF16)


def _fnet(a, b):
    bsz, s, c = a.shape
    n1, n2 = _fnet_factors(s)
    g_tab, f2_tab = _fnet_tables(s)
    tn2 = min(8, n2)
    tk = min(8, n1)
    av = a.reshape(bsz, n1, n2 * c)
    bv = b.reshape(bsz, n1, n2 * c)
    blk = pl.BlockSpec((None, n1, tn2 * c), lambda j, i: (i, 0, j))
    p, q = pl.pallas_call(
        functools.partial(_fnet1_kernel, n1=n1, tn2=tn2, c=c),
        grid=(n2 // tn2, bsz),
        in_specs=[blk, blk, pl.BlockSpec((tn2, 2 * n1, 2 * n1), lambda j, i: (j, 0, 0))],
        out_specs=[blk, blk],
        out_shape=[jax.ShapeDtypeStruct((bsz, n1, n2 * c), BF16)] * 2,
        compiler_params=_cparams("parallel", "parallel"),
        name="fnet_stage1",
    )(av, bv, g_tab)
    pv = p.reshape(bsz, n1 * n2, c)
    qv = q.reshape(bsz, n1 * n2, c)
    rblk = pl.BlockSpec((None, tk * n2, c), lambda i, j: (i, j, 0))
    y = pl.pallas_call(
        functools.partial(_fnet2_kernel, n2=n2, tk=tk, c=c),
        grid=(bsz, n1 // tk),
        in_specs=[rblk, rblk, pl.BlockSpec((n2, 2 * n2), lambda i, j: (0, 0))],
        out_specs=pl.BlockSpec((None, n2, tk * c), lambda i, j: (i, 0, j)),
        out_shape=jax.ShapeDtypeStruct((bsz, n2, n1 * c), BF16),
        compiler_params=_cparams("parallel", "parallel"),
        name="fnet_stage2",
    )(pv, qv, f2_tab)
    return y.reshape(bsz, s, c)


GCHUNK = 128
GHALF = GCHUNK // 2


@functools.lru_cache(maxsize=None)
def _gla_cum_table(reverse):
    n = GCHUNK
    i = np.arange(n)[:, None]
    t = np.arange(n)[None, :]
    tri = (t >= i) if reverse else (t <= i)
    return (tri.astype(np.float32) / GLA_TAU).astype(BF16)


def _dot_split2(a, b):
    p = _split_bf16(b, 2)
    return _dot(a, p[0]) + _dot(a, p[1])


def _gla_scan(q_ref, k_ref, v_ref, g_ref, w2_ref, b_ref, tri_ref, o_ref, state_ref, reverse):
    ts = q_ref.shape[0]
    n = GCHUNK
    nck = ts // n
    dk, dv = GLA_DK_PAD, GLA_DV_PAD
    r = lax.broadcasted_iota(jnp.int32, (n, n), 0)
    c = lax.broadcasted_iota(jnp.int32, (n, n), 1)
    mask = (c >= r) if reverse else (c <= r)
    w2 = w2_ref[...].astype(BF16)
    tri = tri_ref[...]
    mid_row = GHALF if reverse else GHALF - 1
    end_row = 0 if reverse else n - 1

    @pl.when(pl.program_id(1) == 0)
    def _():
        state_ref[...] = jnp.zeros_like(state_ref)

    chunks = range(nck)
    heads = range(GLA_HEADS)
    crow = [slice(c * n, (c + 1) * n) for c in chunks]
    la = _log_sigmoid(_dot(g_ref[...].astype(BF16), w2) + b_ref[...])
    cum = [_dot_split2(tri, la[crow[c], :]) for c in chunks]
    rel = [cum[c] - cum[c][mid_row:mid_row + 1, :] for c in chunks]
    tot = [cum[c][end_row:end_row + 1, :] for c in chunks]
    qf = [q_ref[crow[c], :].astype(F32) for c in chunks]
    kf = [k_ref[crow[c], :].astype(F32) for c in chunks]
    q_mid = [(qf[c] * jnp.exp(rel[c])).astype(BF16) for c in chunks]
    q_in = [(qf[c] * jnp.exp(cum[c])).astype(BF16) for c in chunks]
    k_mid = [kf[c] * jnp.exp(-rel[c]) for c in chunks]
    k_out = [kf[c] * jnp.exp(tot[c] - cum[c]) for c in chunks]
    dec = [jnp.broadcast_to(jnp.exp(tot[c]), (n, GLA_QK_PAD)) for c in chunks]
    ksl = [slice(h * dk, (h + 1) * dk) for h in heads]
    vsl = [slice(h * dv, (h + 1) * dv) for h in heads]
    k_mid_t = [[k_mid[c][:, ksl[h]].T.astype(BF16) for h in heads] for c in chunks]
    k_out_t = [[k_out[c][:, ksl[h]].T.astype(BF16) for h in heads] for c in chunks]
    dec_t = [[dec[c][:, ksl[h]].T for h in heads] for c in chunks]
    att = [[jnp.where(mask, _dot(q_mid[c][:, ksl[h]], k_mid_t[c][h]), 0.0).astype(BF16) for h in heads]
           for c in chunks]
    contrib = [[_dot(k_out_t[c][h], v_ref[crow[c], vsl[h]]) for h in heads] for c in chunks]
    for ci in chunks:
        c = nck - 1 - ci if reverse else ci
        for h in heads:
            st = state_ref[h]
            o_ref[crow[c], vsl[h]] = _dot(jnp.concatenate([att[c][h], q_in[c][:, ksl[h]]], axis=1),
                                          jnp.concatenate([v_ref[crow[c], vsl[h]], st.astype(BF16)], axis=0))
            state_ref[h] = st * jnp.concatenate([dec_t[c][h]] * (dv // n), axis=1) + contrib[c][h]


def _gla_fwd_kernel(q_ref, k_ref, v_ref, g_ref, w2_ref, b_ref, tri_ref, of_ref, state_ref, o_scr):
    _gla_scan(q_ref, k_ref, v_ref, g_ref, w2_ref, b_ref, tri_ref, o_scr, state_ref, False)
    of_ref[...] = o_scr[...].astype(BF16)


def _gla_bwd_kernel(q_ref, k_ref, v_ref, g_ref, w2_ref, b_ref, tri_ref,
                    of_ref, r_ref, yf_ref, h_ref, gn_ref, wf_ref, wg_ref, out_ref, state_ref, o_scr):
    _gla_scan(q_ref, k_ref, v_ref, g_ref, w2_ref, b_ref, tri_ref, o_scr, state_ref, True)
    o = o_scr[...] + of_ref[...].astype(F32)
    normed = []
    for h in range(GLA_HEADS):
        oh = o[:, h * GLA_DV_PAD:(h + 1) * GLA_DV_PAD]
        ms = jnp.sum(oh * oh, axis=-1, keepdims=True) * (1.0 / GLA_DV)
        normed.append(oh * lax.rsqrt(ms + EPS))
    yg = jnp.concatenate(normed, axis=1) * gn_ref[...]
    yg = (yg * jax.nn.silu(r_ref[...].astype(F32))).astype(BF16)
    out_ref[...] = h_ref[...] + _dot(yf_ref[...], wf_ref[...]) + _dot(yg, wg_ref[...])


def _gla(q, k, v, g, w2f, bf, w2b, bb, r, yf, h, gnorm, w_out_f, w_out_g, ts):
    bsz, s, _ = q.shape
    d = h.shape[-1]
    nblk = s // ts
    gw = 2 * GLA_GATE_RANK
    fwd = lambda b, i: (b, i, 0)
    bwd = lambda b, i: (b, nblk - 1 - i, 0)
    const = lambda b, i: (0, 0)

    def seq_specs(rev):
        im = bwd if rev else fwd
        return [pl.BlockSpec((None, ts, GLA_QK_PAD), im), pl.BlockSpec((None, ts, GLA_QK_PAD), im),
                pl.BlockSpec((None, ts, GLA_V_PAD), im), pl.BlockSpec((None, ts, gw), im),
                pl.BlockSpec((gw, GLA_QK_PAD), const), pl.BlockSpec((1, GLA_QK_PAD), const),
                pl.BlockSpec((GCHUNK, GCHUNK), const)]

    scratch = [pltpu.VMEM((GLA_HEADS, GLA_DK_PAD, GLA_DV_PAD), F32), pltpu.VMEM((ts, GLA_V_PAD), F32)]
    o_f = pl.pallas_call(
        _gla_fwd_kernel,
        grid=(bsz, nblk),
        in_specs=seq_specs(False),
        out_specs=pl.BlockSpec((None, ts, GLA_V_PAD), fwd),
        out_shape=jax.ShapeDtypeStruct((bsz, s, GLA_V_PAD), BF16),
        scratch_shapes=scratch,
        compiler_params=_cparams("parallel", "arbitrary"),
        name="gla_fwd",
    )(q, k, v, g, w2f, bf, _gla_cum_table(False))
    return pl.pallas_call(
        _gla_bwd_kernel,
        grid=(bsz, nblk),
        in_specs=seq_specs(True) + [
            pl.BlockSpec((None, ts, GLA_V_PAD), bwd), pl.BlockSpec((None, ts, GLA_V_PAD), bwd),
            pl.BlockSpec((None, ts, FNET_WIDTH), bwd), pl.BlockSpec((None, ts, d), bwd),
            pl.BlockSpec((1, GLA_V_PAD), const), pl.BlockSpec((FNET_WIDTH, d), const),
            pl.BlockSpec((GLA_V_PAD, d), const)],
        out_specs=pl.BlockSpec((None, ts, d), bwd),
        out_shape=jax.ShapeDtypeStruct((bsz, s, d), F32),
        scratch_shapes=scratch,
        compiler_params=_cparams("parallel", "arbitrary"),
        name="gla_bwd_out",
    )(q, k, v, g, w2b, bb, _gla_cum_table(True), o_f, r, yf, h, gnorm, w_out_f, w_out_g)


MCHUNK = 128
assert MCHUNK == MLSTM_DH
MGATE_ROWS = 2 * MLSTM_HEADS
ZROWS = 80
ZCOLS = 3 * MLSTM_DH


@functools.lru_cache(maxsize=None)
def _mlstm_zsel(reverse):
    z = np.zeros((ZROWS, MLSTM_HEADS * ZCOLS), np.float32)
    rbase = MLSTM_HEADS if reverse else 0
    for h in range(MLSTM_HEADS):
        for t in range(3):
            for k in range(3):
                z[(3 * t + k) * MGATE_ROWS + rbase + h, h * ZCOLS + t * MLSTM_DH:h * ZCOLS + (t + 1) * MLSTM_DH] = 1.0
    return z.astype(BF16)


def _cummax_lanes(x, reverse):
    n = x.shape[-1]
    lane = lax.broadcasted_iota(jnp.int32, x.shape, 1)
    s = 1
    while s < n:
        if reverse:
            shifted = jnp.where(lane < n - s, pltpu.roll(x, n - s, axis=1), -jnp.inf)
        else:
            shifted = jnp.where(lane >= s, pltpu.roll(x, s, axis=1), -jnp.inf)
        x = jnp.maximum(x, shifted)
        s *= 2
    return x


def _mlstm_scan(q_ref, kt_ref, v_ref, gi_ref, gf_ref, zsel_ref, o_ref, st_ref, m_ref, reverse):
    ts = q_ref.shape[0]
    n = MCHUNK
    nck = ts // n
    dh = MLSTM_DH
    gr = MGATE_ROWS
    mask, tri_t_bf = _tri_masks(n, reverse)
    rbase = MLSTM_HEADS if reverse else 0

    @pl.when(pl.program_id(1) == 0)
    def _():
        st_ref[...] = jnp.zeros_like(st_ref)
        m_ref[...] = jnp.zeros_like(m_ref)

    gi = gi_ref[...].reshape(nck * gr, n)
    gf = gf_ref[...].reshape(nck * gr, n)
    cum = _cumsum_cols(_log_sigmoid(gf), tri_t_bf)
    b = gi - cum
    md = cum + _cummax_lanes(b, reverse)
    tot = cum[:, 0:1] if reverse else cum[:, n - 1:n]
    w_end = tot + b
    m_loc = jnp.max(w_end, axis=1, keepdims=True)
    e = jnp.exp(w_end - m_loc)
    ones = jnp.ones((n, dh), BF16)
    zpad = jnp.zeros((gr, n), BF16)
    zsel = zsel_ref[...]

    order = [nck - 1 - ci if reverse else ci for ci in range(nck)]
    heads = range(MLSTM_HEADS)
    grow = [slice(c * gr, (c + 1) * gr) for c in range(nck)]
    crow = [slice(c * n, (c + 1) * n) for c in range(nck)]
    hsl = [slice(h * dh, (h + 1) * dh) for h in heads]
    m_prev, fa, fb = [None] * nck, [None] * nck, [None] * nck
    m = m_ref[...]
    for c in order:
        m_prev[c] = m
        m_new = jnp.maximum(tot[grow[c]] + m, m_loc[grow[c]])
        fa[c] = jnp.exp(tot[grow[c]] + m - m_new)
        fb[c] = jnp.exp(m_loc[grow[c]] - m_new)
        m = m_new
    m_ref[...] = m
    z = [None] * nck
    for c in order:
        lw = cum[grow[c]] + m_prev[c][:, :n]
        m_t = jnp.maximum(lw, md[grow[c]])
        pieces = _split_bf16(cum[grow[c]] - m_t, 3) + _split_bf16(lw - m_t, 3) + _split_bf16(-m_t, 3) + [zpad]
        z[c] = _dot_tn(jnp.concatenate(pieces, axis=0), zsel)
    qk = [[_dot(q_ref[crow[c], hsl[h]], kt_ref[hsl[h], crow[c]]) for h in heads] for c in order]
    vaug = [[jnp.concatenate([v_ref[crow[c], hsl[h]], ones], axis=1) for h in heads] for c in order]
    s_qk, qa, floor, contrib = ([[None] * MLSTM_HEADS for _ in order] for _ in range(4))
    for i, c in enumerate(order):
        for h in heads:
            r = c * gr + rbase + h
            zb = h * ZCOLS
            expo = jnp.where(mask, z[c][:, zb:zb + dh] + b[r:r + 1, :], -jnp.inf)
            s_qk[i][h] = (qk[i][h] * jnp.exp(expo)).astype(BF16)
            qa[i][h] = (q_ref[crow[c], hsl[h]].astype(F32) * jnp.exp(z[c][:, zb + dh:zb + 2 * dh])).astype(BF16)
            floor[i][h] = jnp.exp(z[c][:, zb + 2 * dh:zb + 3 * dh])
            ket = (kt_ref[hsl[h], crow[c]].astype(F32) * e[r:r + 1, :]).astype(BF16)
            contrib[i][h] = _dot(ket, vaug[i][h])
    for i, c in enumerate(order):
        for h in heads:
            rr = rbase + h
            st = st_ref[h]
            res = _dot(jnp.concatenate([s_qk[i][h], qa[i][h]], axis=1),
                       jnp.concatenate([vaug[i][h], st.astype(BF16)], axis=0))
            o_ref[crow[c], hsl[h]] = res[:, :dh] / jnp.maximum(jnp.abs(res[:, dh:]), floor[i][h])
            st_ref[h] = fa[c][rr:rr + 1, :] * st + fb[c][rr:rr + 1, :] * contrib[i][h]


def _mlstm_fwd_kernel(q_ref, kt_ref, v_ref, gi_ref, gf_ref, zsel_ref, hf_ref, st_ref, m_ref, o_scr):
    _mlstm_scan(q_ref, kt_ref, v_ref, gi_ref, gf_ref, zsel_ref, o_scr, st_ref, m_ref, False)
    hf_ref[...] = o_scr[...].astype(BF16)


def _mlstm_bwd_kernel(q_ref, kt_ref, v_ref, gi_ref, gf_ref, zsel_ref, hf_ref, og_ref, sb_ref, t_ref, tp_ref, tn_ref,
                      h_ref, mn_ref, cw_ref, cb_ref, wc_ref, wm_ref, out_ref, st_ref, m_ref, o_scr):
    _mlstm_scan(q_ref, kt_ref, v_ref, gi_ref, gf_ref, zsel_ref, o_scr, st_ref, m_ref, True)
    ts = q_ref.shape[0]
    o = o_scr[...] + hf_ref[...].astype(F32)
    normed = []
    for h in range(MLSTM_HEADS):
        oh = o[:, h * MLSTM_DH:(h + 1) * MLSTM_DH]
        normed.append(oh * lax.rsqrt(jnp.mean(oh * oh, axis=-1, keepdims=True) + EPS))
    ym = jnp.concatenate(normed, axis=1) * mn_ref[...]
    ym = (ym * jax.nn.sigmoid(og_ref[...].astype(F32))).astype(BF16)
    i = pl.program_id(1)
    nblk = pl.num_programs(1)
    prev = jnp.where(i < nblk - 1, tp_ref[...].astype(F32), 0.0)
    nxt = jnp.where(i > 0, tn_ref[...].astype(F32), 0.0)
    ext = jnp.concatenate([prev, t_ref[...].astype(F32), nxt], axis=0)
    rows = ts + 2 * HALO_BF16
    lo = pltpu.roll(ext, 1, axis=0)[HALO_BF16:HALO_BF16 + ts]
    hi = pltpu.roll(ext, rows - 1, axis=0)[HALO_BF16:HALO_BF16 + ts]
    cw = cw_ref[...]
    conv = lo * cw[0:1, :] + ext[HALO_BF16:HALO_BF16 + ts] * cw[1:2, :] + hi * cw[2:3, :] + cb_ref[...]
    yc = (sb_ref[...].astype(F32) * conv).astype(BF16)
    out_ref[...] = h_ref[...] + _dot(yc, wc_ref[...]) + _dot(ym, wm_ref[...])


def _mlstm(q, kt, v, gi, gf, og, sb, t, h, mnorm, conv_w, conv_b, w_out_c, w_out_m, ts):
    bsz, s, _ = q.shape
    d = h.shape[-1]
    nblk = s // ts
    nck = ts // MCHUNK
    hb = ts // HALO_BF16
    fwd = lambda b, i: (b, i, 0)
    bwd = lambda b, i: (b, nblk - 1 - i, 0)
    const = lambda b, i: (0, 0)

    def seq_specs(rev):
        w = MLSTM_WIDTH
        blk = (lambda i: nblk - 1 - i) if rev else (lambda i: i)
        im = lambda b, i: (b, blk(i), 0)
        imt = lambda b, i: (b, 0, blk(i))
        im4 = lambda b, i: (b, blk(i), 0, 0)
        gspec = pl.BlockSpec((None, nck, MGATE_ROWS, MCHUNK), im4)
        return [pl.BlockSpec((None, ts, w), im), pl.BlockSpec((None, w, ts), imt), pl.BlockSpec((None, ts, w), im),
                gspec, gspec, pl.BlockSpec((ZROWS, MLSTM_HEADS * ZCOLS), const)]

    scratch = [pltpu.VMEM((MLSTM_HEADS, MLSTM_DH, 2 * MLSTM_DH), F32), pltpu.VMEM((MGATE_ROWS, 2 * MLSTM_DH), F32),
               pltpu.VMEM((ts, MLSTM_WIDTH), F32)]
    h_f = pl.pallas_call(
        _mlstm_fwd_kernel,
        grid=(bsz, nblk),
        in_specs=seq_specs(False),
        out_specs=pl.BlockSpec((None, ts, MLSTM_WIDTH), fwd),
        out_shape=jax.ShapeDtypeStruct((bsz, s, MLSTM_WIDTH), BF16),
        scratch_shapes=scratch,
        compiler_params=_cparams("parallel", "arbitrary"),
        name="mlstm_fwd",
    )(q, kt, v, gi, gf, _mlstm_zsel(False))
    last16 = s // HALO_BF16 - 1
    halo_prev = lambda b, i: (b, jnp.maximum((nblk - 1 - i) * hb - 1, 0), 0)
    halo_next = lambda b, i: (b, jnp.minimum((nblk - i) * hb, last16), 0)
    return pl.pallas_call(
        _mlstm_bwd_kernel,
        grid=(bsz, nblk),
        in_specs=seq_specs(True) + [
            pl.BlockSpec((None, ts, MLSTM_WIDTH), bwd), pl.BlockSpec((None, ts, MLSTM_WIDTH), bwd),
            pl.BlockSpec((None, ts, CONV_WIDTH), bwd), pl.BlockSpec((None, ts, CONV_WIDTH), bwd),
            pl.BlockSpec((None, HALO_BF16, CONV_WIDTH), halo_prev),
            pl.BlockSpec((None, HALO_BF16, CONV_WIDTH), halo_next),
            pl.BlockSpec((None, ts, d), bwd),
            pl.BlockSpec((1, MLSTM_WIDTH), const), pl.BlockSpec((3, CONV_WIDTH), const),
            pl.BlockSpec((1, CONV_WIDTH), const), pl.BlockSpec((CONV_WIDTH, d), const),
            pl.BlockSpec((MLSTM_WIDTH, d), const)],
        out_specs=pl.BlockSpec((None, ts, d), bwd),
        out_shape=jax.ShapeDtypeStruct((bsz, s, d), F32),
        scratch_shapes=scratch,
        compiler_params=_cparams("parallel", "arbitrary"),
        name="mlstm_bwd_out",
    )(q, kt, v, gi, gf, _mlstm_zsel(True), h_f, og, sb, t, t, t, h, mnorm, conv_w, conv_b, w_out_c, w_out_m)


def _ffn_kernel(h_ref, hp_ref, hn_ref, p_ref, fg_ref, wu_ref, cw_ref, cb_ref, wd_ref, pg_ref, pgw_ref, pw_ref,
                fn_ref, out_ref, xn_scr, act_scr, *, d_ff, f_tile, final):
    ts = h_ref.shape[0]
    i = pl.program_id(1)
    nblk = pl.num_programs(1)
    x = h_ref[...]
    fg = fg_ref[...]
    xn_scr[0:HALO, :] = jnp.where(i > 0, _rms(hp_ref[...], fg), 0.0).astype(BF16)
    xn_scr[HALO:HALO + ts, :] = _rms(x, fg).astype(BF16)
    xn_scr[HALO + ts:, :] = jnp.where(i < nblk - 1, _rms(hn_ref[...], fg), 0.0).astype(BF16)
    rows = ts + 2 * HALO
    xe = xn_scr[...]
    xc = xe[HALO:HALO + ts]
    for f0 in range(0, d_ff, f_tile):
        gate = _dot(xe, wu_ref[:, f0:f0 + f_tile])
        val = _dot(xc, wu_ref[:, d_ff + f0:d_ff + f0 + f_tile])
        cw = cw_ref[:, f0:f0 + f_tile]
        lo = pltpu.roll(gate, 1, axis=0)[HALO:HALO + ts]
        hi = pltpu.roll(gate, rows - 1, axis=0)[HALO:HALO + ts]
        conv = (lo * cw[0:1, :] + gate[HALO:HALO + ts] * cw[1:2, :] + hi * cw[2:3, :]
                + cb_ref[:, f0:f0 + f_tile])
        act_scr[:, f0:f0 + f_tile] = (jax.nn.silu(conv) * val).astype(BF16)
    h2 = x + _dot(act_scr[...], wd_ref[...])
    gate = jax.nn.sigmoid(_dot(_rms(h2, pg_ref[...]).astype(BF16), pgw_ref[...]))
    h3 = h2 + _dot(p_ref[...].astype(BF16), pw_ref[...]) * gate
    out_ref[...] = _rms(h3, fn_ref[...]) if final else h3


def _ffn(h, p, layer, fgain, w_up, conv_w, conv_b, w_down, pgain, pg_w, p_w, final_gain, final, ts, f_tile):
    bsz, s, d = h.shape
    d_ff = w_down.shape[0]
    nblk = s // ts
    hb = ts // HALO
    last8 = s // HALO - 1
    blk = lambda b, i: (b, i, 0)
    const = lambda b, i: (0, 0)
    once = pl.Buffered(1)

    def wspec(shape):
        return pl.BlockSpec(shape, const, pipeline_mode=once)

    return pl.pallas_call(
        functools.partial(_ffn_kernel, d_ff=d_ff, f_tile=f_tile, final=final),
        grid=(bsz, nblk),
        in_specs=[pl.BlockSpec((None, ts, d), blk),
                  pl.BlockSpec((None, HALO, d), lambda b, i: (b, jnp.maximum(i * hb - 1, 0), 0)),
                  pl.BlockSpec((None, HALO, d), lambda b, i: (b, jnp.minimum((i + 1) * hb, last8), 0)),
                  pl.BlockSpec((None, None, ts, p.shape[-1]), lambda b, i: (layer, b, i, 0)),
                  wspec((1, d)), wspec((d, 2 * d_ff)), wspec((3, d_ff)), wspec((1, d_ff)), wspec((d_ff, d)),
                  wspec((1, d)), wspec((d, d)), wspec((p.shape[-1], d)), wspec((1, d))],
        out_specs=pl.BlockSpec((None, ts, d), blk),
        out_shape=jax.ShapeDtypeStruct((bsz, s, d), F32),
        scratch_shapes=[pltpu.VMEM((ts + 2 * HALO, d), BF16), pltpu.VMEM((ts, d_ff), BF16)],
        compiler_params=_cparams("parallel", "parallel"),
        name="ffn_ple",
    )(h, h, h, p, fgain, w_up, conv_w, conv_b, w_down, pgain, pg_w, p_w, final_gain)


def _pad_heads_cols(w, heads, dh, dpad):
    lead = w.shape[:-1]
    w = w.reshape(lead + (heads, dh))
    w = jnp.pad(w, [(0, 0)] * len(lead) + [(0, 0), (0, dpad - dh)])
    return w.reshape(lead + (heads * dpad,))


def _pad_cols(w, width):
    return jnp.pad(w, [(0, 0)] * (w.ndim - 1) + [(0, width - w.shape[-1])])


def _even_weights(w_in, w2_f, b_f, w2_b, b_b, gnorm, w_out):
    cuts = np.cumsum([FNET_WIDTH, GLA_HEADS * GLA_DK, GLA_HEADS * GLA_DK, GLA_HEADS * GLA_DV,
                      GLA_HEADS * GLA_DV]).tolist()
    wu, wq, wk, wv, wr, wg = jnp.split(w_in, cuts, axis=-1)
    w_pad = jnp.concatenate([
        wu, _pad_heads_cols(wq * (GLA_DK ** -0.5), GLA_HEADS, GLA_DK, GLA_DK_PAD),
        _pad_heads_cols(wk, GLA_HEADS, GLA_DK, GLA_DK_PAD),
        _pad_heads_cols(wv, GLA_HEADS, GLA_DV, GLA_DV_PAD), _pad_heads_cols(wr, GLA_HEADS, GLA_DV, GLA_DV_PAD),
        _pad_cols(wg, GATE_PAD)], axis=-1).astype(BF16)
    zero = jnp.zeros((GLA_GATE_RANK, GLA_QK_PAD), F32)
    w2f = jnp.concatenate([_pad_heads_cols(w2_f, GLA_HEADS, GLA_DK, GLA_DK_PAD), zero], axis=0)
    w2b = jnp.concatenate([zero, _pad_heads_cols(w2_b, GLA_HEADS, GLA_DK, GLA_DK_PAD)], axis=0)
    bf = _pad_heads_cols(b_f[None, :], GLA_HEADS, GLA_DK, GLA_DK_PAD)
    bb = _pad_heads_cols(b_b[None, :], GLA_HEADS, GLA_DK, GLA_DK_PAD)
    gn = _pad_heads_cols(gnorm[None, :], GLA_HEADS, GLA_DV, GLA_DV_PAD)
    w_out_f = w_out[:FNET_WIDTH].astype(BF16)
    wog = w_out[FNET_WIDTH:].reshape(GLA_HEADS, GLA_DV, -1)
    wog = jnp.pad(wog, ((0, 0), (0, GLA_DV_PAD - GLA_DV), (0, 0))).reshape(GLA_V_PAD, -1).astype(BF16)
    return w_pad, w2f, bf, w2b, bb, gn, w_out_f, wog


def _trunk(x, p, e_norm, e_w_in, e_gla_w2_f, e_gla_b_f, e_gla_w2_b, e_gla_b_b, e_gla_norm, e_w_out,
           o_norm, o_w_in, o_conv_w, o_conv_b, o_gate_bias, o_mlstm_norm, o_w_out,
           ffn_norm, ffn_w_up, ffn_conv_w, ffn_conv_b, ffn_w_down,
           ple_w, ple_gate_norm, ple_gate_w, final_norm, *, ts=ROW_TILE, f_tile=256):
    bsz, s, d = x.shape
    depth = p.shape[0]
    t = bsz * s
    ts = min(ts, s)
    h = x
    for layer in range(depth):
        j = layer // 2
        h2d = h.reshape(t, d)
        if layer % 2 == 0:
            w_pad, w2f, bf, w2b, bb, gn, w_out_f, w_out_g = _even_weights(
                e_w_in[j], e_gla_w2_f[j], e_gla_b_f[j], e_gla_w2_b[j], e_gla_b_b[j], e_gla_norm[j], e_w_out[j])
            a, b, q, k, v, r, g = _inproj_even(h2d, e_norm[j][None, :], w_pad, _channel_dft(), min(2 * ts, s))
            sh = lambda z: z.reshape(bsz, s, z.shape[-1])
            yf = _fnet(sh(a), sh(b))
            h = _gla(sh(q), sh(k), sh(v), sh(g), w2f, bf, w2b, bb, sh(r), yf, h, gn, w_out_f, w_out_g, min(2 * ts, s))
        else:
            q_lo = 3 * CONV_WIDTH
            col = jnp.arange(o_w_in.shape[-1])
            qscale = jnp.where((col >= q_lo) & (col < q_lo + MLSTM_WIDTH), MLSTM_DH ** -0.5, 1.0)
            w_pad = _pad_cols(o_w_in[j] * qscale, ODD_W).astype(BF16)
            sb, tt, q, k, v, og, gates = _inproj_odd(h2d, o_norm[j][None, :], w_pad, o_gate_bias[j][None, :], min(2 * ts, s))
            sh = lambda z: z.reshape(bsz, s, z.shape[-1])
            nh = MLSTM_HEADS
            gates = gates.reshape(bsz, s // MCHUNK, MCHUNK, ODD_GATES).transpose(0, 1, 3, 2)
            gi = jnp.concatenate([gates[:, :, 0:nh], gates[:, :, 2 * nh:3 * nh]], axis=2)
            gf = jnp.concatenate([gates[:, :, nh:2 * nh], gates[:, :, 3 * nh:4 * nh]], axis=2)
            w_out = o_w_out[j].astype(BF16)
            h = _mlstm(sh(q), sh(k).transpose(0, 2, 1), sh(v), gi, gf, sh(og), sh(sb), sh(tt), h,
                       o_mlstm_norm[j][None, :], o_conv_w[j], o_conv_b[j][None, :],
                       w_out[:CONV_WIDTH], w_out[CONV_WIDTH:], min(2 * ts, s))
        h = _ffn(h, p, layer, ffn_norm[layer][None, :], ffn_w_up[layer].astype(BF16), ffn_conv_w[layer],
                 ffn_conv_b[layer][None, :], ffn_w_down[layer].astype(BF16), ple_gate_norm[layer][None, :],
                 ple_gate_w[layer].astype(BF16), ple_w[layer].astype(BF16), final_norm[None, :],
                 layer == depth - 1, min(2 * ts, s), f_tile)
    return h


def kernel(x_prompt, x_sample, p_prompt, p_sample, e_norm, e_w_in, e_gla_w2_f, e_gla_b_f, e_gla_w2_b, e_gla_b_b, e_gla_norm, e_w_out, o_norm, o_w_in, o_conv_w, o_conv_b, o_gate_bias, o_mlstm_norm, o_w_out, ffn_norm, ffn_w_up, ffn_conv_w, ffn_conv_b, ffn_w_down, ple_w, ple_gate_norm, ple_gate_w, final_norm):
    weights = (e_norm, e_w_in, e_gla_w2_f, e_gla_b_f, e_gla_w2_b, e_gla_b_b, e_gla_norm, e_w_out,
               o_norm, o_w_in, o_conv_w, o_conv_b, o_gate_bias, o_mlstm_norm, o_w_out,
               ffn_norm, ffn_w_up, ffn_conv_w, ffn_conv_b, ffn_w_down,
               ple_w, ple_gate_norm, ple_gate_w, final_norm)
    return (_trunk(x_prompt, p_prompt, *weights), _trunk(x_sample, p_sample, *weights))
```

```python
import functools
import math

import numpy as np
import jax
import jax.numpy as jnp
from jax import lax
from jax.experimental import pallas as pl
from jax.experimental.pallas import tpu as pltpu

F32 = jnp.float32
BF16 = jnp.bfloat16
EPS = 1e-6

FNET_GROUPS = 4
FNET_GROUP_DIM = 64
FNET_WIDTH = FNET_GROUPS * FNET_GROUP_DIM
GLA_HEADS = 4
GLA_DK = 96
GLA_DV = 192
GLA_GATE_RANK = 16
GLA_TAU = 16.0
CONV_WIDTH = 512
MLSTM_HEADS = 4
MLSTM_DH = 128
MLSTM_WIDTH = MLSTM_HEADS * MLSTM_DH

GLA_DK_PAD = 128
GLA_DV_PAD = 256
GLA_QK_PAD = GLA_HEADS * GLA_DK_PAD
GLA_V_PAD = GLA_HEADS * GLA_DV_PAD
GATE_PAD = 128

V7X_VMEM_BYTES = 64 * 1024 * 1024
VMEM_LIMIT = V7X_VMEM_BYTES - 8 * 1024 * 1024

ROW_TILE = 512
HALO = 8
HALO_BF16 = 16


def _cparams(*sem):
    return pltpu.CompilerParams(dimension_semantics=sem, vmem_limit_bytes=VMEM_LIMIT)


def _rms(x, g):
    ms = jnp.mean(x * x, axis=-1, keepdims=True)
    return x * lax.rsqrt(ms + EPS) * g


def _split_bf16(x, n):
    parts = []
    r = x
    for _ in range(n):
        p = r.astype(BF16)
        parts.append(p)
        r = r - p.astype(F32)
    return parts


def _dot(a, b):
    return jnp.dot(a, b, preferred_element_type=F32)


LOG2E = 1.4426950408889634


def _log2_sigmoid(x):
    return jnp.minimum(x, 0.0) * LOG2E - jnp.log2(1.0 + jnp.exp2(jnp.abs(x) * (-LOG2E)))


def _dot_tn(a, b):
    return lax.dot_general(a, b, (((0,), (0,)), ((), ())), preferred_element_type=F32)


def _tri_masks(n, reverse):
    r = lax.broadcasted_iota(jnp.int32, (n, n), 0)
    c = lax.broadcasted_iota(jnp.int32, (n, n), 1)
    mask = (c >= r) if reverse else (c <= r)
    return mask, ((r >= c) if reverse else (r <= c)).astype(BF16)


def _cumsum_cols(x, tri_t_bf):
    acc = None
    for p in _split_bf16(x, 3):
        t = _dot(p, tri_t_bf)
        acc = t if acc is None else acc + t
    return acc


INPROJ_SUB = 512
EVEN_SEGS = (FNET_WIDTH, GLA_QK_PAD, GLA_QK_PAD, GLA_V_PAD, GLA_V_PAD, GATE_PAD)
EVEN_W = sum(EVEN_SEGS)


def _inproj_even_kernel(h_ref, g_ref, w_ref, dft_ref, a_ref, b_ref, q_ref, k_ref, v_ref, r_ref, gg_ref):
    offs = np.cumsum((0,) + EVEN_SEGS)
    tm = h_ref.shape[0]
    for r0 in range(0, tm, INPROJ_SUB):
        rows = slice(r0, r0 + INPROJ_SUB)
        xn = _rms(h_ref[rows, :], g_ref[...]).astype(BF16)

        def seg(i):
            return _dot(xn, w_ref[:, offs[i]:offs[i + 1]])

        u = seg(0).astype(BF16)
        ab = _dot(u, dft_ref[...])
        a_ref[rows, :] = ab[:, :FNET_WIDTH]
        b_ref[rows, :] = ab[:, FNET_WIDTH:]
        q_ref[rows, :] = seg(1).astype(BF16)
        k_ref[rows, :] = seg(2).astype(BF16)
        v_ref[rows, :] = seg(3).astype(BF16)
        r_ref[rows, :] = seg(4).astype(BF16)
        gg_ref[rows, :] = seg(5)[:, :2 * GLA_GATE_RANK]


def _inproj_even(h2d, gain, w_pad, dft_cs, tm):
    t, d = h2d.shape
    row = lambda i: (i, 0)
    const = lambda i: (0, 0)
    widths = (FNET_WIDTH, FNET_WIDTH, GLA_QK_PAD, GLA_QK_PAD, GLA_V_PAD, GLA_V_PAD, 2 * GLA_GATE_RANK)
    dts = (F32, F32) + (BF16,) * 4 + (F32,)
    return pl.pallas_call(
        _inproj_even_kernel,
        grid=(t // tm,),
        in_specs=[pl.BlockSpec((tm, d), row), pl.BlockSpec((1, d), const),
                  pl.BlockSpec((d, EVEN_W), const), pl.BlockSpec((FNET_WIDTH, 2 * FNET_WIDTH), const)],
        out_specs=[pl.BlockSpec((tm, w), row) for w in widths],
        out_shape=[jax.ShapeDtypeStruct((t, w), dt) for w, dt in zip(widths, dts)],
        compiler_params=_cparams("parallel"),
        name="inproj_even",
    )(h2d, gain, w_pad, dft_cs)


ODD_GATES = 4 * MLSTM_HEADS
ODD_W = 3 * CONV_WIDTH + 4 * MLSTM_WIDTH + GATE_PAD


def _inproj_odd_kernel(h_ref, g_ref, w_ref, gb_ref, sb_ref, t_ref, q_ref, k_ref, v_ref, og_ref, gt_ref):
    cw = CONV_WIDTH
    base = 3 * cw
    tm = h_ref.shape[0]
    for r0 in range(0, tm, INPROJ_SUB):
        rows = slice(r0, r0 + INPROJ_SUB)
        xn = _rms(h_ref[rows, :], g_ref[...]).astype(BF16)

        def seg(lo, n):
            return _dot(xn, w_ref[:, lo:lo + n])

        sb_ref[rows, :] = seg(0, cw).astype(BF16)
        t_ref[rows, :] = (seg(cw, cw) * seg(2 * cw, cw)).astype(BF16)
        q_ref[rows, :] = seg(base, MLSTM_WIDTH).astype(BF16)
        k_ref[rows, :] = seg(base + MLSTM_WIDTH, MLSTM_WIDTH).astype(BF16)
        v_ref[rows, :] = seg(base + 2 * MLSTM_WIDTH, MLSTM_WIDTH).astype(BF16)
        og_ref[rows, :] = seg(base + 3 * MLSTM_WIDTH, MLSTM_WIDTH).astype(BF16)
        gt_ref[rows, :] = seg(base + 4 * MLSTM_WIDTH, GATE_PAD)[:, :ODD_GATES] + gb_ref[...]


def _inproj_odd(h2d, gain, w_pad, gate_bias, tm):
    t, d = h2d.shape
    row = lambda i: (i, 0)
    const = lambda i: (0, 0)
    widths = (CONV_WIDTH, CONV_WIDTH) + (MLSTM_WIDTH,) * 4 + (ODD_GATES,)
    dts = (BF16,) * 6 + (F32,)
    return pl.pallas_call(
        _inproj_odd_kernel,
        grid=(t // tm,),
        in_specs=[pl.BlockSpec((tm, d), row), pl.BlockSpec((1, d), const),
                  pl.BlockSpec((d, ODD_W), const), pl.BlockSpec((1, ODD_GATES), const)],
        out_specs=[pl.BlockSpec((tm, w), row) for w in widths],
        out_shape=[jax.ShapeDtypeStruct((t, w), dt) for w, dt in zip(widths, dts)],
        compiler_params=_cparams("parallel"),
        name="inproj_odd",
    )(h2d, gain, w_pad, gate_bias)


def _fnet_factors(s):
    lg = int(round(math.log2(s)))
    assert 2 ** lg == s
    n1 = 2 ** ((lg + 1) // 2)
    return n1, s // n1


@functools.lru_cache(maxsize=None)
def _fnet_tables(s):
    n1, n2 = _fnet_factors(s)
    k1 = np.arange(n1, dtype=np.int64)[None, :, None]
    m1 = np.arange(n1, dtype=np.int64)[None, None, :]
    j2 = np.arange(n2, dtype=np.int64)[:, None, None]
    ang = 2.0 * np.pi * ((k1 * (j2 + n2 * m1)) % s).astype(np.float64) / s
    gc = np.cos(ang) / np.sqrt(n1)
    gs = np.sin(ang) / np.sqrt(n1)
    g = np.concatenate([np.concatenate([gc, -gs], axis=2), np.concatenate([-gs, -gc], axis=2)], axis=1)
    k2 = np.arange(n2, dtype=np.int64)[:, None]
    m2 = np.arange(n2, dtype=np.int64)[None, :]
    ang2 = 2.0 * np.pi * ((k2 * m2) % n2).astype(np.float64) / n2
    f2 = np.concatenate([np.cos(ang2), np.sin(ang2)], axis=1) / np.sqrt(n2)
    return g.astype(np.float32).astype(BF16), f2.astype(np.float32).astype(BF16)


@functools.lru_cache(maxsize=None)
def _channel_dft():
    j = np.arange(FNET_WIDTH)
    same = (j[:, None] // FNET_GROUP_DIM) == (j[None, :] // FNET_GROUP_DIM)
    ang = 2.0 * np.pi * (((j[:, None] % FNET_GROUP_DIM) * (j[None, :] % FNET_GROUP_DIM)) % FNET_GROUP_DIM) / FNET_GROUP_DIM
    c = np.where(same, np.cos(ang), 0.0) / np.sqrt(FNET_GROUP_DIM)
    sn = np.where(same, np.sin(ang), 0.0) / np.sqrt(FNET_GROUP_DIM)
    return np.concatenate([c, sn], axis=1).astype(np.float32).astype(BF16)


FNET_T = 8


def _fnet1_kernel(a_ref, b_ref, g_ref, p_ref, q_ref, *, n1):
    for t in range(FNET_T):
        rhs = jnp.concatenate([a_ref[:, t, :], b_ref[:, t, :]], axis=0).astype(BF16)
        res = _dot(g_ref[t], rhs)
        p_ref[:, t, :] = res[:n1]
        q_ref[:, t, :] = res[n1:]


def _fnet2_kernel(p_ref, q_ref, f_ref, y_ref, *, n2):
    for t in range(FNET_T):
        rs = slice(t * n2, (t + 1) * n2)
        rhs = jnp.concatenate([p_ref[rs, :], q_ref[rs, :]], axis=0).astype(BF16)
        y_ref[:, t, :] = _dot(f_ref[...], rhs)


def _fnet(a, b):
    bsz, s, c = a.shape
    n1, n2 = _fnet_factors(s)
    g_tab, f2_tab = _fnet_tables(s)
    t = FNET_T
    blk = pl.BlockSpec((None, n1, t, c), lambda j, i: (i, 0, j, 0))
    p, q = pl.pallas_call(
        functools.partial(_fnet1_kernel, n1=n1),
        grid=(n2 // t, bsz),
        in_specs=[blk, blk, pl.BlockSpec((t, 2 * n1, 2 * n1), lambda j, i: (j, 0, 0))],
        out_specs=[blk, blk],
        out_shape=[jax.ShapeDtypeStruct((bsz, n1, n2, c), F32)] * 2,
        compiler_params=_cparams("parallel", "parallel"),
        name="fnet_stage1",
    )(a.reshape(bsz, n1, n2, c), b.reshape(bsz, n1, n2, c), g_tab)
    rblk = pl.BlockSpec((None, t * n2, c), lambda i, j: (i, j, 0))
    y = pl.pallas_call(
        functools.partial(_fnet2_kernel, n2=n2),
        grid=(bsz, n1 // t),
        in_specs=[rblk, rblk, pl.BlockSpec((n2, 2 * n2), lambda i, j: (0, 0))],
        out_specs=pl.BlockSpec((None, n2, t, c), lambda i, j: (i, 0, j, 0)),
        out_shape=jax.ShapeDtypeStruct((bsz, n2, n1, c), F32),
        compiler_params=_cparams("parallel", "parallel"),
        name="fnet_stage2",
    )(p.reshape(bsz, s, c), q.reshape(bsz, s, c), f2_tab)
    return y.reshape(bsz, s, c)


GCHUNK = 128
EPILOGUE_SUB = 256
SCAN_GROUP = 4
GHALF = GCHUNK // 2


@functools.lru_cache(maxsize=None)
def _gla_cum_table(reverse):
    n = GCHUNK
    i = np.arange(n)[:, None]
    t = np.arange(n)[None, :]
    tri = (t >= i) if reverse else (t <= i)
    return (tri.astype(np.float32) / GLA_TAU).astype(BF16)


def _dot_split2(a, b):
    p = _split_bf16(b, 2)
    return _dot(a, p[0]) + _dot(a, p[1])


def _gla_scan(q_ref, k_ref, v_ref, g_ref, w2_ref, b_ref, tri_ref, o_ref, state_ref, reverse):
    ts = q_ref.shape[0]
    n = GCHUNK
    nck = ts // n
    dk, dv = GLA_DK_PAD, GLA_DV_PAD
    r = lax.broadcasted_iota(jnp.int32, (n, n), 0)
    c = lax.broadcasted_iota(jnp.int32, (n, n), 1)
    mask = (c >= r) if reverse else (c <= r)
    w2 = w2_ref[...].astype(BF16)
    tri = tri_ref[...]
    mid_row = GHALF if reverse else GHALF - 1
    end_row = 0 if reverse else n - 1

    @pl.when(pl.program_id(1) == 0)
    def _():
        state_ref[...] = jnp.zeros_like(state_ref)

    heads = range(GLA_HEADS)
    ksl = [slice(h * dk, (h + 1) * dk) for h in heads]
    vsl = [slice(h * dv, (h + 1) * dv) for h in heads]
    order = [nck - 1 - ci if reverse else ci for ci in range(nck)]
    for g0 in range(0, nck, SCAN_GROUP):
        grp = order[g0:g0 + SCAN_GROUP]
        idx = range(len(grp))
        crow = [slice(c * n, (c + 1) * n) for c in grp]
        la = [_log2_sigmoid(_dot(g_ref[crow[i], :].astype(BF16), w2) + b_ref[...]) for i in idx]
        cum = [_dot_split2(tri, la[i]) for i in idx]
        rel = [cum[i] - cum[i][mid_row:mid_row + 1, :] for i in idx]
        tot = [cum[i][end_row:end_row + 1, :] for i in idx]
        qf = [q_ref[crow[i], :].astype(F32) for i in idx]
        kf = [k_ref[crow[i], :].astype(F32) for i in idx]
        q_mid = [(qf[i] * jnp.exp2(rel[i])).astype(BF16) for i in idx]
        q_in = [(qf[i] * jnp.exp2(cum[i])).astype(BF16) for i in idx]
        k_mid = [kf[i] * jnp.exp2(-rel[i]) for i in idx]
        k_out = [kf[i] * jnp.exp2(tot[i] - cum[i]) for i in idx]
        dec = [jnp.broadcast_to(jnp.exp2(tot[i]), (n, GLA_QK_PAD)) for i in idx]
        k_mid_t = [[k_mid[i][:, ksl[h]].T.astype(BF16) for h in heads] for i in idx]
        k_out_t = [[k_out[i][:, ksl[h]].T.astype(BF16) for h in heads] for i in idx]
        dec_t = [[dec[i][:, ksl[h]].T for h in heads] for i in idx]
        att = [[jnp.where(mask, _dot(q_mid[i][:, ksl[h]], k_mid_t[i][h]), 0.0).astype(BF16) for h in heads]
               for i in idx]
        contrib = [[_dot(k_out_t[i][h], v_ref[crow[i], vsl[h]]) for h in heads] for i in idx]
        for i in idx:
            for h in heads:
                st = state_ref[h]
                o_ref[crow[i], vsl[h]] = _dot(jnp.concatenate([att[i][h], q_in[i][:, ksl[h]]], axis=1),
                                              jnp.concatenate([v_ref[crow[i], vsl[h]], st.astype(BF16)], axis=0))
                state_ref[h] = st * jnp.concatenate([dec_t[i][h]] * (dv // n), axis=1) + contrib[i][h]


def _gla_fwd_kernel(q_ref, k_ref, v_ref, g_ref, w2_ref, b_ref, tri_ref, of_ref, state_ref, o_scr):
    _gla_scan(q_ref, k_ref, v_ref, g_ref, w2_ref, b_ref, tri_ref, o_scr, state_ref, False)
    of_ref[...] = o_scr[...].astype(BF16)


def _gla_bwd_kernel(q_ref, k_ref, v_ref, g_ref, w2_ref, b_ref, tri_ref,
                    of_ref, r_ref, yf_ref, h_ref, gn_ref, wf_ref, wg_ref, out_ref, state_ref, o_scr):
    _gla_scan(q_ref, k_ref, v_ref, g_ref, w2_ref, b_ref, tri_ref, o_scr, state_ref, True)
    ts = q_ref.shape[0]
    for r0 in range(0, ts, min(EPILOGUE_SUB, ts)):
        rs = slice(r0, r0 + min(EPILOGUE_SUB, ts))
        o = o_scr[rs, :] + of_ref[rs, :].astype(F32)
        normed = []
        for h in range(GLA_HEADS):
            oh = o[:, h * GLA_DV_PAD:(h + 1) * GLA_DV_PAD]
            ms = jnp.sum(oh * oh, axis=-1, keepdims=True) * (1.0 / GLA_DV)
            normed.append(oh * lax.rsqrt(ms + EPS))
        yg = jnp.concatenate(normed, axis=1) * gn_ref[...]
        yg = (yg * jax.nn.silu(r_ref[rs, :].astype(F32))).astype(BF16)
        out_ref[rs, :] = h_ref[rs, :] + _dot(yf_ref[rs, :].astype(BF16), wf_ref[...]) + _dot(yg, wg_ref[...])


def _gla(q, k, v, g, w2f, bf, w2b, bb, r, yf, h, gnorm, w_out_f, w_out_g, ts):
    bsz, s, _ = q.shape
    d = h.shape[-1]
    nblk = s // ts
    gw = 2 * GLA_GATE_RANK
    fwd = lambda b, i: (b, i, 0)
    bwd = lambda b, i: (b, nblk - 1 - i, 0)
    const = lambda b, i: (0, 0)

    def seq_specs(rev):
        im = bwd if rev else fwd
        return [pl.BlockSpec((None, ts, GLA_QK_PAD), im), pl.BlockSpec((None, ts, GLA_QK_PAD), im),
                pl.BlockSpec((None, ts, GLA_V_PAD), im), pl.BlockSpec((None, ts, gw), im),
                pl.BlockSpec((gw, GLA_QK_PAD), const), pl.BlockSpec((1, GLA_QK_PAD), const),
                pl.BlockSpec((GCHUNK, GCHUNK), const)]

    scratch = [pltpu.VMEM((GLA_HEADS, GLA_DK_PAD, GLA_DV_PAD), F32), pltpu.VMEM((ts, GLA_V_PAD), F32)]
    o_f = pl.pallas_call(
        _gla_fwd_kernel,
        grid=(bsz, nblk),
        in_specs=seq_specs(False),
        out_specs=pl.BlockSpec((None, ts, GLA_V_PAD), fwd),
        out_shape=jax.ShapeDtypeStruct((bsz, s, GLA_V_PAD), BF16),
        scratch_shapes=scratch,
        compiler_params=_cparams("parallel", "arbitrary"),
        name="gla_fwd",
    )(q, k, v, g, w2f, bf, _gla_cum_table(False))
    return pl.pallas_call(
        _gla_bwd_kernel,
        grid=(bsz, nblk),
        in_specs=seq_specs(True) + [
            pl.BlockSpec((None, ts, GLA_V_PAD), bwd), pl.BlockSpec((None, ts, GLA_V_PAD), bwd),
            pl.BlockSpec((None, ts, FNET_WIDTH), bwd), pl.BlockSpec((None, ts, d), bwd),
            pl.BlockSpec((1, GLA_V_PAD), const), pl.BlockSpec((FNET_WIDTH, d), const),
            pl.BlockSpec((GLA_V_PAD, d), const)],
        out_specs=pl.BlockSpec((None, ts, d), bwd),
        out_shape=jax.ShapeDtypeStruct((bsz, s, d), F32),
        scratch_shapes=scratch,
        compiler_params=_cparams("parallel", "arbitrary"),
        name="gla_bwd_out",
    )(q, k, v, g, w2b, bb, _gla_cum_table(True), o_f, r, yf, h, gnorm, w_out_f, w_out_g)


MCHUNK = 128
assert MCHUNK == MLSTM_DH
MGATE_ROWS = 2 * MLSTM_HEADS
ZROWS = 80
ZCOLS = 3 * MLSTM_DH


@functools.lru_cache(maxsize=None)
def _mlstm_zsel(reverse):
    z = np.zeros((ZROWS, MLSTM_HEADS * ZCOLS), np.float32)
    rbase = MLSTM_HEADS if reverse else 0
    for h in range(MLSTM_HEADS):
        for t in range(3):
            for k in range(3):
                z[(3 * t + k) * MGATE_ROWS + rbase + h, h * ZCOLS + t * MLSTM_DH:h * ZCOLS + (t + 1) * MLSTM_DH] = 1.0
    return z.astype(BF16)


def _cummax_lanes(x, reverse):
    n = x.shape[-1]
    lane = lax.broadcasted_iota(jnp.int32, x.shape, 1)
    s = 1
    while s < n:
        if reverse:
            shifted = jnp.where(lane < n - s, pltpu.roll(x, n - s, axis=1), -jnp.inf)
        else:
            shifted = jnp.where(lane >= s, pltpu.roll(x, s, axis=1), -jnp.inf)
        x = jnp.maximum(x, shifted)
        s *= 2
    return x


def _mlstm_scan(q_ref, kt_ref, v_ref, gi_ref, gf_ref, zsel_ref, o_ref, st_ref, m_ref, reverse):
    ts = q_ref.shape[0]
    n = MCHUNK
    nck = ts // n
    dh = MLSTM_DH
    gr = MGATE_ROWS
    mask, tri_t_bf = _tri_masks(n, reverse)
    rbase = MLSTM_HEADS if reverse else 0

    @pl.when(pl.program_id(1) == 0)
    def _():
        st_ref[...] = jnp.zeros_like(st_ref)
        m_ref[...] = jnp.zeros_like(m_ref)

    gi = gi_ref[...].reshape(nck * gr, n) * LOG2E
    gf = gf_ref[...].reshape(nck * gr, n)
    cum = _cumsum_cols(_log2_sigmoid(gf), tri_t_bf)
    b = gi - cum
    md = cum + _cummax_lanes(b, reverse)
    tot = cum[:, 0:1] if reverse else cum[:, n - 1:n]
    w_end = tot + b
    m_loc = jnp.max(w_end, axis=1, keepdims=True)
    e = jnp.exp2(w_end - m_loc)
    ones = jnp.ones((n, dh), BF16)
    zpad = jnp.zeros((gr, n), BF16)
    zsel = zsel_ref[...]

    order = [nck - 1 - ci if reverse else ci for ci in range(nck)]
    heads = range(MLSTM_HEADS)
    grow = [slice(c * gr, (c + 1) * gr) for c in range(nck)]
    crow = [slice(c * n, (c + 1) * n) for c in range(nck)]
    hsl = [slice(h * dh, (h + 1) * dh) for h in heads]
    m_prev, fa, fb = [None] * nck, [None] * nck, [None] * nck
    m = m_ref[...]
    for c in order:
        m_prev[c] = m
        m_new = jnp.maximum(tot[grow[c]] + m, m_loc[grow[c]])
        fa[c] = jnp.exp2(tot[grow[c]] + m - m_new)
        fb[c] = jnp.exp2(m_loc[grow[c]] - m_new)
        m = m_new
    m_ref[...] = m
    z = [None] * nck
    for c in order:
        lw = cum[grow[c]] + m_prev[c][:, :n]
        m_t = jnp.maximum(lw, md[grow[c]])
        pieces = _split_bf16(cum[grow[c]] - m_t, 3) + _split_bf16(lw - m_t, 3) + _split_bf16(-m_t, 3) + [zpad]
        z[c] = _dot_tn(jnp.concatenate(pieces, axis=0), zsel)
    qk = [[_dot(q_ref[crow[c], hsl[h]], kt_ref[hsl[h], crow[c]]) for h in heads] for c in order]
    vaug = [[jnp.concatenate([v_ref[crow[c], hsl[h]], ones], axis=1) for h in heads] for c in order]
    s_qk, qa, floor, contrib = ([[None] * MLSTM_HEADS for _ in order] for _ in range(4))
    for i, c in enumerate(order):
        for h in heads:
            r = c * gr + rbase + h
            zb = h * ZCOLS
            expo = jnp.where(mask, z[c][:, zb:zb + dh] + b[r:r + 1, :], -jnp.inf)
            s_qk[i][h] = (qk[i][h] * jnp.exp2(expo)).astype(BF16)
            qa[i][h] = (q_ref[crow[c], hsl[h]].astype(F32) * jnp.exp2(z[c][:, zb + dh:zb + 2 * dh])).astype(BF16)
            floor[i][h] = jnp.exp2(z[c][:, zb + 2 * dh:zb + 3 * dh])
            ket = (kt_ref[hsl[h], crow[c]].astype(F32) * e[r:r + 1, :]).astype(BF16)
            contrib[i][h] = _dot(ket, vaug[i][h])
    for i, c in enumerate(order):
        for h in heads:
            rr = rbase + h
            st = st_ref[h]
            res = _dot(jnp.concatenate([s_qk[i][h], qa[i][h]], axis=1),
                       jnp.concatenate([vaug[i][h], st.astype(BF16)], axis=0))
            o_ref[crow[c], hsl[h]] = res[:, :dh] / jnp.maximum(jnp.abs(res[:, dh:]), floor[i][h])
            st_ref[h] = fa[c][rr:rr + 1, :] * st + fb[c][rr:rr + 1, :] * contrib[i][h]


def _mlstm_fwd_kernel(q_ref, kt_ref, v_ref, gi_ref, gf_ref, zsel_ref, hf_ref, st_ref, m_ref, o_scr):
    _mlstm_scan(q_ref, kt_ref, v_ref, gi_ref, gf_ref, zsel_ref, o_scr, st_ref, m_ref, False)
    hf_ref[...] = o_scr[...].astype(BF16)


def _mlstm_bwd_kernel(q_ref, kt_ref, v_ref, gi_ref, gf_ref, zsel_ref, hf_ref, og_ref, sb_ref, t_ref, tp_ref, tn_ref,
                      h_ref, mn_ref, cw_ref, cb_ref, wc_ref, wm_ref, out_ref, st_ref, m_ref, o_scr):
    _mlstm_scan(q_ref, kt_ref, v_ref, gi_ref, gf_ref, zsel_ref, o_scr, st_ref, m_ref, True)
    ts = q_ref.shape[0]
    i = pl.program_id(1)
    nblk = pl.num_programs(1)
    prev = jnp.where(i < nblk - 1, tp_ref[...].astype(F32), 0.0)
    nxt = jnp.where(i > 0, tn_ref[...].astype(F32), 0.0)
    ext = jnp.concatenate([prev, t_ref[...].astype(F32), nxt], axis=0)
    rows = ts + 2 * HALO_BF16
    lo = pltpu.roll(ext, 1, axis=0)[HALO_BF16:HALO_BF16 + ts]
    hi = pltpu.roll(ext, rows - 1, axis=0)[HALO_BF16:HALO_BF16 + ts]
    cw = cw_ref[...]
    conv = lo * cw[0:1, :] + ext[HALO_BF16:HALO_BF16 + ts] * cw[1:2, :] + hi * cw[2:3, :] + cb_ref[...]
    yc = (sb_ref[...].astype(F32) * conv).astype(BF16)
    for r0 in range(0, ts, min(EPILOGUE_SUB, ts)):
        rs = slice(r0, r0 + min(EPILOGUE_SUB, ts))
        o = o_scr[rs, :] + hf_ref[rs, :].astype(F32)
        normed = []
        for h in range(MLSTM_HEADS):
            oh = o[:, h * MLSTM_DH:(h + 1) * MLSTM_DH]
            normed.append(oh * lax.rsqrt(jnp.mean(oh * oh, axis=-1, keepdims=True) + EPS))
        ym = jnp.concatenate(normed, axis=1) * mn_ref[...]
        ym = (ym * jax.nn.sigmoid(og_ref[rs, :].astype(F32))).astype(BF16)
        out_ref[rs, :] = h_ref[rs, :] + _dot(yc[rs, :], wc_ref[...]) + _dot(ym, wm_ref[...])


def _mlstm(q, kt, v, gi, gf, og, sb, t, h, mnorm, conv_w, conv_b, w_out_c, w_out_m, ts):
    bsz, s, _ = q.shape
    d = h.shape[-1]
    nblk = s // ts
    nck = ts // MCHUNK
    hb = ts // HALO_BF16
    fwd = lambda b, i: (b, i, 0)
    bwd = lambda b, i: (b, nblk - 1 - i, 0)
    const = lambda b, i: (0, 0)

    def seq_specs(rev):
        w = MLSTM_WIDTH
        blk = (lambda i: nblk - 1 - i) if rev else (lambda i: i)
        im = lambda b, i: (b, blk(i), 0)
        imt = lambda b, i: (b, 0, blk(i))
        im4 = lambda b, i: (b, blk(i), 0, 0)
        gspec = pl.BlockSpec((None, nck, MGATE_ROWS, MCHUNK), im4)
        return [pl.BlockSpec((None, ts, w), im), pl.BlockSpec((None, w, ts), imt), pl.BlockSpec((None, ts, w), im),
                gspec, gspec, pl.BlockSpec((ZROWS, MLSTM_HEADS * ZCOLS), const)]

    scratch = [pltpu.VMEM((MLSTM_HEADS, MLSTM_DH, 2 * MLSTM_DH), F32), pltpu.VMEM((MGATE_ROWS, 2 * MLSTM_DH), F32),
               pltpu.VMEM((ts, MLSTM_WIDTH), F32)]
    h_f = pl.pallas_call(
        _mlstm_fwd_kernel,
        grid=(bsz, nblk),
        in_specs=seq_specs(False),
        out_specs=pl.BlockSpec((None, ts, MLSTM_WIDTH), fwd),
        out_shape=jax.ShapeDtypeStruct((bsz, s, MLSTM_WIDTH), BF16),
        scratch_shapes=scratch,
        compiler_params=_cparams("parallel", "arbitrary"),
        name="mlstm_fwd",
    )(q, kt, v, gi, gf, _mlstm_zsel(False))
    last16 = s // HALO_BF16 - 1
    halo_prev = lambda b, i: (b, jnp.maximum((nblk - 1 - i) * hb - 1, 0), 0)
    halo_next = lambda b, i: (b, jnp.minimum((nblk - i) * hb, last16), 0)
    return pl.pallas_call(
        _mlstm_bwd_kernel,
        grid=(bsz, nblk),
        in_specs=seq_specs(True) + [
            pl.BlockSpec((None, ts, MLSTM_WIDTH), bwd), pl.BlockSpec((None, ts, MLSTM_WIDTH), bwd),
            pl.BlockSpec((None, ts, CONV_WIDTH), bwd), pl.BlockSpec((None, ts, CONV_WIDTH), bwd),
            pl.BlockSpec((None, HALO_BF16, CONV_WIDTH), halo_prev),
            pl.BlockSpec((None, HALO_BF16, CONV_WIDTH), halo_next),
            pl.BlockSpec((None, ts, d), bwd),
            pl.BlockSpec((1, MLSTM_WIDTH), const), pl.BlockSpec((3, CONV_WIDTH), const),
            pl.BlockSpec((1, CONV_WIDTH), const), pl.BlockSpec((CONV_WIDTH, d), const),
            pl.BlockSpec((MLSTM_WIDTH, d), const)],
        out_specs=pl.BlockSpec((None, ts, d), bwd),
        out_shape=jax.ShapeDtypeStruct((bsz, s, d), F32),
        scratch_shapes=scratch,
        compiler_params=_cparams("parallel", "arbitrary"),
        name="mlstm_bwd_out",
    )(q, kt, v, gi, gf, _mlstm_zsel(True), h_f, og, sb, t, t, t, h, mnorm, conv_w, conv_b, w_out_c, w_out_m)


FFN_SUB = 1024


def _ffn_kernel(h_ref, hp_ref, hn_ref, p_ref, fg_ref, wu_ref, cw_ref, cb_ref, wd_ref, pg_ref, pgw_ref, pw_ref,
                fn_ref, out_ref, xn_scr, act_scr, *, d_ff, f_tile, final):
    ts = h_ref.shape[0]
    i = pl.program_id(1)
    nblk = pl.num_programs(1)
    fg = fg_ref[...]
    sub = min(FFN_SUB, ts)
    rows = sub + 2 * HALO
    for k, r0 in enumerate(range(0, ts, sub)):
        x = h_ref[r0:r0 + sub, :]
        if r0 == 0:
            prev = jnp.where(i > 0, _rms(hp_ref[...], fg), 0.0)
        else:
            prev = _rms(h_ref[r0 - HALO:r0, :], fg)
        if r0 + sub == ts:
            nxt = jnp.where(i < nblk - 1, _rms(hn_ref[...], fg), 0.0)
        else:
            nxt = _rms(h_ref[r0 + sub:r0 + sub + HALO, :], fg)
        xn_scr[k, 0:HALO, :] = prev.astype(BF16)
        xn_scr[k, HALO:HALO + sub, :] = _rms(x, fg).astype(BF16)
        xn_scr[k, HALO + sub:, :] = nxt.astype(BF16)
        xe = xn_scr[k]
        xc = xe[HALO:HALO + sub]
        for f0 in range(0, d_ff, f_tile):
            gate = _dot(xe, wu_ref[:, f0:f0 + f_tile])
            val = _dot(xc, wu_ref[:, d_ff + f0:d_ff + f0 + f_tile])
            cw = cw_ref[:, f0:f0 + f_tile]
            lo = pltpu.roll(gate, 1, axis=0)[HALO:HALO + sub]
            hi = pltpu.roll(gate, rows - 1, axis=0)[HALO:HALO + sub]
            conv = (lo * cw[0:1, :] + gate[HALO:HALO + sub] * cw[1:2, :] + hi * cw[2:3, :]
                    + cb_ref[:, f0:f0 + f_tile])
            act_scr[r0:r0 + sub, f0:f0 + f_tile] = (jax.nn.silu(conv) * val).astype(BF16)
        h2 = x + _dot(act_scr[r0:r0 + sub, :], wd_ref[...])
        gate = jax.nn.sigmoid(_dot(_rms(h2, pg_ref[...]).astype(BF16), pgw_ref[...]))
        h3 = h2 + _dot(p_ref[r0:r0 + sub, :].astype(BF16), pw_ref[...]) * gate
        out_ref[r0:r0 + sub, :] = _rms(h3, fn_ref[...]) if final else h3


def _ffn(h, p, layer, fgain, w_up, conv_w, conv_b, w_down, pgain, pg_w, p_w, final_gain, final, ts, f_tile):
    bsz, s, d = h.shape
    d_ff = w_down.shape[0]
    nblk = s // ts
    hb = ts // HALO
    last8 = s // HALO - 1
    blk = lambda b, i: (b, i, 0)
    const = lambda b, i: (0, 0)
    once = pl.Buffered(1)

    def wspec(shape):
        return pl.BlockSpec(shape, const, pipeline_mode=once)

    return pl.pallas_call(
        functools.partial(_ffn_kernel, d_ff=d_ff, f_tile=f_tile, final=final),
        grid=(bsz, nblk),
        in_specs=[pl.BlockSpec((None, ts, d), blk),
                  pl.BlockSpec((None, HALO, d), lambda b, i: (b, jnp.maximum(i * hb - 1, 0), 0)),
                  pl.BlockSpec((None, HALO, d), lambda b, i: (b, jnp.minimum((i + 1) * hb, last8), 0)),
                  pl.BlockSpec((None, None, ts, p.shape[-1]), lambda b, i: (layer, b, i, 0)),
                  wspec((1, d)), wspec((d, 2 * d_ff)), wspec((3, d_ff)), wspec((1, d_ff)), wspec((d_ff, d)),
                  wspec((1, d)), wspec((d, d)), wspec((p.shape[-1], d)), wspec((1, d))],
        out_specs=pl.BlockSpec((None, ts, d), blk),
        out_shape=jax.ShapeDtypeStruct((bsz, s, d), F32),
        scratch_shapes=[pltpu.VMEM((-(-ts // FFN_SUB), min(FFN_SUB, ts) + 2 * HALO, d), BF16),
                        pltpu.VMEM((ts, d_ff), BF16)],
        compiler_params=_cparams("parallel", "parallel"),
        name="ffn_ple",
    )(h, h, h, p, fgain, w_up, conv_w, conv_b, w_down, pgain, pg_w, p_w, final_gain)


def _pad_heads_cols(w, heads, dh, dpad):
    lead = w.shape[:-1]
    w = w.reshape(lead + (heads, dh))
    w = jnp.pad(w, [(0, 0)] * len(lead) + [(0, 0), (0, dpad - dh)])
    return w.reshape(lead + (heads * dpad,))


def _pad_cols(w, width):
    return jnp.pad(w, [(0, 0)] * (w.ndim - 1) + [(0, width - w.shape[-1])])


def _even_weights(w_in, w2_f, b_f, w2_b, b_b, gnorm, w_out):
    cuts = np.cumsum([FNET_WIDTH, GLA_HEADS * GLA_DK, GLA_HEADS * GLA_DK, GLA_HEADS * GLA_DV,
                      GLA_HEADS * GLA_DV]).tolist()
    wu, wq, wk, wv, wr, wg = jnp.split(w_in, cuts, axis=-1)
    w_pad = jnp.concatenate([
        wu, _pad_heads_cols(wq * (GLA_DK ** -0.5), GLA_HEADS, GLA_DK, GLA_DK_PAD),
        _pad_heads_cols(wk, GLA_HEADS, GLA_DK, GLA_DK_PAD),
        _pad_heads_cols(wv, GLA_HEADS, GLA_DV, GLA_DV_PAD), _pad_heads_cols(wr, GLA_HEADS, GLA_DV, GLA_DV_PAD),
        _pad_cols(wg, GATE_PAD)], axis=-1).astype(BF16)
    zero = jnp.zeros((GLA_GATE_RANK, GLA_QK_PAD), F32)
    w2f = jnp.concatenate([_pad_heads_cols(w2_f, GLA_HEADS, GLA_DK, GLA_DK_PAD), zero], axis=0)
    w2b = jnp.concatenate([zero, _pad_heads_cols(w2_b, GLA_HEADS, GLA_DK, GLA_DK_PAD)], axis=0)
    bf = _pad_heads_cols(b_f[None, :], GLA_HEADS, GLA_DK, GLA_DK_PAD)
    bb = _pad_heads_cols(b_b[None, :], GLA_HEADS, GLA_DK, GLA_DK_PAD)
    gn = _pad_heads_cols(gnorm[None, :], GLA_HEADS, GLA_DV, GLA_DV_PAD)
    w_out_f = w_out[:FNET_WIDTH].astype(BF16)
    wog = w_out[FNET_WIDTH:].reshape(GLA_HEADS, GLA_DV, -1)
    wog = jnp.pad(wog, ((0, 0), (0, GLA_DV_PAD - GLA_DV), (0, 0))).reshape(GLA_V_PAD, -1).astype(BF16)
    return w_pad, w2f, bf, w2b, bb, gn, w_out_f, wog


def _trunk(x, p, e_norm, e_w_in, e_gla_w2_f, e_gla_b_f, e_gla_w2_b, e_gla_b_b, e_gla_norm, e_w_out,
           o_norm, o_w_in, o_conv_w, o_conv_b, o_gate_bias, o_mlstm_norm, o_w_out,
           ffn_norm, ffn_w_up, ffn_conv_w, ffn_conv_b, ffn_w_down,
           ple_w, ple_gate_norm, ple_gate_w, final_norm, *, ts=ROW_TILE, f_tile=256):
    bsz, s, d = x.shape
    depth = p.shape[0]
    t = bsz * s
    ts = min(ts, s)
    h = x
    for layer in range(depth):
        j = layer // 2
        h2d = h.reshape(t, d)
        if layer % 2 == 0:
            w_pad, w2f, bf, w2b, bb, gn, w_out_f, w_out_g = _even_weights(
                e_w_in[j], e_gla_w2_f[j], e_gla_b_f[j], e_gla_w2_b[j], e_gla_b_b[j], e_gla_norm[j], e_w_out[j])
            a, b, q, k, v, r, g = _inproj_even(h2d, e_norm[j][None, :], w_pad, _channel_dft(), min(2 * ts, s))
            sh = lambda z: z.reshape(bsz, s, z.shape[-1])
            yf = _fnet(sh(a), sh(b))
            h = _gla(sh(q), sh(k), sh(v), sh(g), w2f, bf, w2b, bb, sh(r), yf, h, gn, w_out_f, w_out_g, min(2 * ts, s))
        else:
            q_lo = 3 * CONV_WIDTH
            col = jnp.arange(o_w_in.shape[-1])
            qscale = jnp.where((col >= q_lo) & (col < q_lo + MLSTM_WIDTH), MLSTM_DH ** -0.5, 1.0)
            w_pad = _pad_cols(o_w_in[j] * qscale, ODD_W).astype(BF16)
            sb, tt, q, k, v, og, gates = _inproj_odd(h2d, o_norm[j][None, :], w_pad, o_gate_bias[j][None, :], min(2 * ts, s))
            sh = lambda z: z.reshape(bsz, s, z.shape[-1])
            nh = MLSTM_HEADS
            gates = gates.reshape(bsz, s // MCHUNK, MCHUNK, ODD_GATES).transpose(0, 1, 3, 2)
            gi = jnp.concatenate([gates[:, :, 0:nh], gates[:, :, 2 * nh:3 * nh]], axis=2)
            gf = jnp.concatenate([gates[:, :, nh:2 * nh], gates[:, :, 3 * nh:4 * nh]], axis=2)
            w_out = o_w_out[j].astype(BF16)
            h = _mlstm(sh(q), sh(k).transpose(0, 2, 1), sh(v), gi, gf, sh(og), sh(sb), sh(tt), h,
                       o_mlstm_norm[j][None, :], o_conv_w[j], o_conv_b[j][None, :],
                       w_out[:CONV_WIDTH], w_out[CONV_WIDTH:], min(2 * ts, s))
        h = _ffn(h, p, layer, ffn_norm[layer][None, :], ffn_w_up[layer].astype(BF16), ffn_conv_w[layer],
                 ffn_conv_b[layer][None, :], ffn_w_down[layer].astype(BF16), ple_gate_norm[layer][None, :],
                 ple_gate_w[layer].astype(BF16), ple_w[layer].astype(BF16), final_norm[None, :],
                 layer == depth - 1, min(2 * ts, s), f_tile)
    return h


def kernel(x_prompt, x_sample, p_prompt, p_sample, e_norm, e_w_in, e_gla_w2_f, e_gla_b_f, e_gla_w2_b, e_gla_b_b, e_gla_norm, e_w_out, o_norm, o_w_in, o_conv_w, o_conv_b, o_gate_bias, o_mlstm_norm, o_w_out, ffn_norm, ffn_w_up, ffn_conv_w, ffn_conv_b, ffn_w_down, ple_w, ple_gate_norm, ple_gate_w, final_norm):
    weights = (e_norm, e_w_in, e_gla_w2_f, e_gla_b_f, e_gla_w2_b, e_gla_b_b, e_gla_norm, e_w_out,
               o_norm, o_w_in, o_conv_w, o_conv_b, o_gate_bias, o_mlstm_norm, o_w_out,
               ffn_norm, ffn_w_up, ffn_conv_w, ffn_conv_b, ffn_w_down,
               ple_w, ple_gate_norm, ple_gate_w, final_norm)
    return (_trunk(x_prompt, p_prompt, *weights), _trunk(x_sample, p_sample, *weights))
```

```python
import functools
import math

import numpy as np
import jax
import jax.numpy as jnp
from jax import lax
from jax.experimental import pallas as pl
from jax.experimental.pallas import tpu as pltpu

F32 = jnp.float32
BF16 = jnp.bfloat16
EPS = 1e-6

FNET_GROUPS = 4
FNET_GROUP_DIM = 64
FNET_WIDTH = FNET_GROUPS * FNET_GROUP_DIM
GLA_HEADS = 4
GLA_DK = 96
GLA_DV = 192
GLA_GATE_RANK = 16
GLA_TAU = 16.0
CONV_WIDTH = 512
MLSTM_HEADS = 4
MLSTM_DH = 128
MLSTM_WIDTH = MLSTM_HEADS * MLSTM_DH

GLA_DK_PAD = 128
GLA_DV_PAD = 256
GLA_QK_PAD = GLA_HEADS * GLA_DK_PAD
GLA_V_PAD = GLA_HEADS * GLA_DV_PAD
GATE_PAD = 128

V7X_VMEM_BYTES = 64 * 1024 * 1024
VMEM_LIMIT = V7X_VMEM_BYTES - 8 * 1024 * 1024

ROW_TILE = 1024
FFN_TILE = 256
HALO = 8
HALO_BF16 = 16


def _cparams(*sem):
    return pltpu.CompilerParams(dimension_semantics=sem, vmem_limit_bytes=VMEM_LIMIT)


def _rms(x, g):
    ms = jnp.mean(x * x, axis=-1, keepdims=True)
    return x * lax.rsqrt(ms + EPS) * g


def _split_bf16(x, n):
    parts = []
    r = x
    for _ in range(n):
        p = r.astype(BF16)
        parts.append(p)
        r = r - p.astype(F32)
    return parts


def _dot(a, b):
    return jnp.dot(a, b, preferred_element_type=F32)


LOG2E = 1.4426950408889634


def _log2_sigmoid(x):
    return jnp.minimum(x, 0.0) * LOG2E - jnp.log2(1.0 + jnp.exp2(jnp.abs(x) * (-LOG2E)))


def _dot_tn(a, b):
    return lax.dot_general(a, b, (((0,), (0,)), ((), ())), preferred_element_type=F32)


def _tri_masks(n, reverse):
    r = lax.broadcasted_iota(jnp.int32, (n, n), 0)
    c = lax.broadcasted_iota(jnp.int32, (n, n), 1)
    mask = (c >= r) if reverse else (c <= r)
    return mask, ((r >= c) if reverse else (r <= c)).astype(BF16)


def _cumsum_cols(x, tri_t_bf):
    acc = None
    for p in _split_bf16(x, 3):
        t = _dot(p, tri_t_bf)
        acc = t if acc is None else acc + t
    return acc


INPROJ_SUB = 512
EVEN_SEGS = (FNET_WIDTH, GLA_QK_PAD, GLA_QK_PAD, GLA_V_PAD, GLA_V_PAD, GATE_PAD)
EVEN_W = sum(EVEN_SEGS)


def _inproj_even_kernel(h_ref, g_ref, w_ref, dft_ref, a_ref, b_ref, q_ref, k_ref, v_ref, r_ref, gg_ref):
    offs = np.cumsum((0,) + EVEN_SEGS)
    tm = h_ref.shape[0]
    for r0 in range(0, tm, INPROJ_SUB):
        rows = slice(r0, r0 + INPROJ_SUB)
        xn = _rms(h_ref[rows, :], g_ref[...]).astype(BF16)

        def seg(i):
            return _dot(xn, w_ref[:, offs[i]:offs[i + 1]])

        u = seg(0).astype(BF16)
        ab = _dot(u, dft_ref[...])
        a_ref[rows, :] = ab[:, :FNET_WIDTH]
        b_ref[rows, :] = ab[:, FNET_WIDTH:]
        q_ref[rows, :] = seg(1).astype(BF16)
        k_ref[rows, :] = seg(2).astype(BF16)
        v_ref[rows, :] = seg(3).astype(BF16)
        r_ref[rows, :] = seg(4).astype(BF16)
        gg_ref[rows, :] = seg(5)[:, :2 * GLA_GATE_RANK]


def _inproj_even(h2d, gain, w_pad, dft_cs, tm):
    t, d = h2d.shape
    row = lambda i: (i, 0)
    const = lambda i: (0, 0)
    widths = (FNET_WIDTH, FNET_WIDTH, GLA_QK_PAD, GLA_QK_PAD, GLA_V_PAD, GLA_V_PAD, 2 * GLA_GATE_RANK)
    dts = (F32, F32) + (BF16,) * 4 + (F32,)
    return pl.pallas_call(
        _inproj_even_kernel,
        grid=(t // tm,),
        in_specs=[pl.BlockSpec((tm, d), row), pl.BlockSpec((1, d), const),
                  pl.BlockSpec((d, EVEN_W), const), pl.BlockSpec((FNET_WIDTH, 2 * FNET_WIDTH), const)],
        out_specs=[pl.BlockSpec((tm, w), row) for w in widths],
        out_shape=[jax.ShapeDtypeStruct((t, w), dt) for w, dt in zip(widths, dts)],
        compiler_params=_cparams("parallel"),
        name="inproj_even",
    )(h2d, gain, w_pad, dft_cs)


ODD_GATES = 4 * MLSTM_HEADS
ODD_W = 3 * CONV_WIDTH + 4 * MLSTM_WIDTH + GATE_PAD


def _inproj_odd_kernel(h_ref, g_ref, w_ref, gb_ref, sb_ref, t_ref, q_ref, k_ref, v_ref, og_ref, gt_ref):
    cw = CONV_WIDTH
    base = 3 * cw
    tm = h_ref.shape[0]
    for r0 in range(0, tm, INPROJ_SUB):
        rows = slice(r0, r0 + INPROJ_SUB)
        xn = _rms(h_ref[rows, :], g_ref[...]).astype(BF16)

        def seg(lo, n):
            return _dot(xn, w_ref[:, lo:lo + n])

        sb_ref[rows, :] = seg(0, cw).astype(BF16)
        t_ref[rows, :] = (seg(cw, cw) * seg(2 * cw, cw)).astype(BF16)
        q_ref[rows, :] = seg(base, MLSTM_WIDTH).astype(BF16)
        k_ref[rows, :] = seg(base + MLSTM_WIDTH, MLSTM_WIDTH).astype(BF16)
        v_ref[rows, :] = seg(base + 2 * MLSTM_WIDTH, MLSTM_WIDTH).astype(BF16)
        og_ref[rows, :] = seg(base + 3 * MLSTM_WIDTH, MLSTM_WIDTH).astype(BF16)
        gt_ref[rows, :] = seg(base + 4 * MLSTM_WIDTH, GATE_PAD)[:, :ODD_GATES] + gb_ref[...]


def _inproj_odd(h2d, gain, w_pad, gate_bias, tm):
    t, d = h2d.shape
    row = lambda i: (i, 0)
    const = lambda i: (0, 0)
    widths = (CONV_WIDTH, CONV_WIDTH) + (MLSTM_WIDTH,) * 4 + (ODD_GATES,)
    dts = (BF16,) * 6 + (F32,)
    return pl.pallas_call(
        _inproj_odd_kernel,
        grid=(t // tm,),
        in_specs=[pl.BlockSpec((tm, d), row), pl.BlockSpec((1, d), const),
                  pl.BlockSpec((d, ODD_W), const), pl.BlockSpec((1, ODD_GATES), const)],
        out_specs=[pl.BlockSpec((tm, w), row) for w in widths],
        out_shape=[jax.ShapeDtypeStruct((t, w), dt) for w, dt in zip(widths, dts)],
        compiler_params=_cparams("parallel"),
        name="inproj_odd",
    )(h2d, gain, w_pad, gate_bias)


def _fnet_factors(s):
    lg = int(round(math.log2(s)))
    assert 2 ** lg == s
    n1 = 2 ** ((lg + 1) // 2)
    return n1, s // n1


@functools.lru_cache(maxsize=None)
def _fnet_tables(s):
    n1, n2 = _fnet_factors(s)
    k1 = np.arange(n1, dtype=np.int64)[None, :, None]
    m1 = np.arange(n1, dtype=np.int64)[None, None, :]
    j2 = np.arange(n2, dtype=np.int64)[:, None, None]
    ang = 2.0 * np.pi * ((k1 * (j2 + n2 * m1)) % s).astype(np.float64) / s
    gc = np.cos(ang) / np.sqrt(n1)
    gs = np.sin(ang) / np.sqrt(n1)
    g = np.concatenate([np.concatenate([gc, -gs], axis=2), np.concatenate([-gs, -gc], axis=2)], axis=1)
    k2 = np.arange(n2, dtype=np.int64)[:, None]
    m2 = np.arange(n2, dtype=np.int64)[None, :]
    ang2 = 2.0 * np.pi * ((k2 * m2) % n2).astype(np.float64) / n2
    f2 = np.concatenate([np.cos(ang2), np.sin(ang2)], axis=1) / np.sqrt(n2)
    return g.astype(np.float32).astype(BF16), f2.astype(np.float32).astype(BF16)


@functools.lru_cache(maxsize=None)
def _channel_dft():
    j = np.arange(FNET_WIDTH)
    same = (j[:, None] // FNET_GROUP_DIM) == (j[None, :] // FNET_GROUP_DIM)
    ang = 2.0 * np.pi * (((j[:, None] % FNET_GROUP_DIM) * (j[None, :] % FNET_GROUP_DIM)) % FNET_GROUP_DIM) / FNET_GROUP_DIM
    c = np.where(same, np.cos(ang), 0.0) / np.sqrt(FNET_GROUP_DIM)
    sn = np.where(same, np.sin(ang), 0.0) / np.sqrt(FNET_GROUP_DIM)
    return np.concatenate([c, sn], axis=1).astype(np.float32).astype(BF16)


FNET_T = 16


def _fnet1_kernel(a_ref, b_ref, g_ref, p_ref, q_ref, *, n1):
    for t in range(FNET_T):
        rhs = jnp.concatenate([a_ref[:, t, :], b_ref[:, t, :]], axis=0).astype(BF16)
        res = _dot(g_ref[t], rhs)
        p_ref[:, t, :] = res[:n1]
        q_ref[:, t, :] = res[n1:]


def _fnet2_kernel(p_ref, q_ref, f_ref, y_ref, *, n2):
    for t in range(FNET_T):
        rs = slice(t * n2, (t + 1) * n2)
        rhs = jnp.concatenate([p_ref[rs, :], q_ref[rs, :]], axis=0).astype(BF16)
        y_ref[:, t, :] = _dot(f_ref[...], rhs)


def _fnet(a, b):
    bsz, s, c = a.shape
    n1, n2 = _fnet_factors(s)
    g_tab, f2_tab = _fnet_tables(s)
    t = FNET_T
    assert n1 % t == 0 and n2 % t == 0
    blk = pl.BlockSpec((None, n1, t, c), lambda j, i: (i, 0, j, 0))
    p, q = pl.pallas_call(
        functools.partial(_fnet1_kernel, n1=n1),
        grid=(n2 // t, bsz),
        in_specs=[blk, blk, pl.BlockSpec((t, 2 * n1, 2 * n1), lambda j, i: (j, 0, 0))],
        out_specs=[blk, blk],
        out_shape=[jax.ShapeDtypeStruct((bsz, n1, n2, c), F32)] * 2,
        compiler_params=_cparams("parallel", "parallel"),
        name="fnet_stage1",
    )(a.reshape(bsz, n1, n2, c), b.reshape(bsz, n1, n2, c), g_tab)
    rblk = pl.BlockSpec((None, t * n2, c), lambda i, j: (i, j, 0))
    y = pl.pallas_call(
        functools.partial(_fnet2_kernel, n2=n2),
        grid=(bsz, n1 // t),
        in_specs=[rblk, rblk, pl.BlockSpec((n2, 2 * n2), lambda i, j: (0, 0))],
        out_specs=pl.BlockSpec((None, n2, t, c), lambda i, j: (i, 0, j, 0)),
        out_shape=jax.ShapeDtypeStruct((bsz, n2, n1, c), F32),
        compiler_params=_cparams("parallel", "parallel"),
        name="fnet_stage2",
    )(p.reshape(bsz, s, c), q.reshape(bsz, s, c), f2_tab)
    return y.reshape(bsz, s, c)


GCHUNK = 128
EPILOGUE_SUB = 256
SCAN_GROUP = 4
GHALF = GCHUNK // 2


@functools.lru_cache(maxsize=None)
def _gla_cum_table(reverse):
    n = GCHUNK
    i = np.arange(n)[:, None]
    t = np.arange(n)[None, :]
    tri = (t >= i) if reverse else (t <= i)
    return (tri.astype(np.float32) / GLA_TAU).astype(BF16)


def _dot_split2(a, b):
    p = _split_bf16(b, 2)
    return _dot(a, p[0]) + _dot(a, p[1])


def _gla_scan(q_ref, k_ref, v_ref, g_ref, w2_ref, b_ref, tri_ref, o_ref, state_ref, reverse):
    ts = q_ref.shape[0]
    n = GCHUNK
    nck = ts // n
    dk, dv = GLA_DK_PAD, GLA_DV_PAD
    r = lax.broadcasted_iota(jnp.int32, (n, n), 0)
    c = lax.broadcasted_iota(jnp.int32, (n, n), 1)
    mask = (c >= r) if reverse else (c <= r)
    w2 = w2_ref[...].astype(BF16)
    tri = tri_ref[...]
    mid_row = GHALF if reverse else GHALF - 1
    end_row = 0 if reverse else n - 1

    @pl.when(pl.program_id(1) == 0)
    def _():
        state_ref[...] = jnp.zeros_like(state_ref)

    heads = range(GLA_HEADS)
    ksl = [slice(h * dk, (h + 1) * dk) for h in heads]
    vsl = [slice(h * dv, (h + 1) * dv) for h in heads]
    order = [nck - 1 - ci if reverse else ci for ci in range(nck)]
    for g0 in range(0, nck, SCAN_GROUP):
        grp = order[g0:g0 + SCAN_GROUP]
        idx = range(len(grp))
        crow = [slice(c * n, (c + 1) * n) for c in grp]
        la = [_log2_sigmoid(_dot(g_ref[crow[i], :].astype(BF16), w2) + b_ref[...]) for i in idx]
        cum = [_dot_split2(tri, la[i]) for i in idx]
        rel = [cum[i] - cum[i][mid_row:mid_row + 1, :] for i in idx]
        tot = [cum[i][end_row:end_row + 1, :] for i in idx]
        qf = [q_ref[crow[i], :].astype(F32) for i in idx]
        kf = [k_ref[crow[i], :].astype(F32) for i in idx]
        q_mid = [(qf[i] * jnp.exp2(rel[i])).astype(BF16) for i in idx]
        q_in = [(qf[i] * jnp.exp2(cum[i])).astype(BF16) for i in idx]
        k_mid = [kf[i] * jnp.exp2(-rel[i]) for i in idx]
        k_out = [kf[i] * jnp.exp2(tot[i] - cum[i]) for i in idx]
        dec = [jnp.broadcast_to(jnp.exp2(tot[i]), (n, GLA_QK_PAD)) for i in idx]
        k_mid_t = [[k_mid[i][:, ksl[h]].T.astype(BF16) for h in heads] for i in idx]
        k_out_t = [[k_out[i][:, ksl[h]].T.astype(BF16) for h in heads] for i in idx]
        dec_t = [[dec[i][:, ksl[h]].T for h in heads] for i in idx]
        att = [[jnp.where(mask, _dot(q_mid[i][:, ksl[h]], k_mid_t[i][h]), 0.0).astype(BF16) for h in heads]
               for i in idx]
        contrib = [[_dot(k_out_t[i][h], v_ref[crow[i], vsl[h]]) for h in heads] for i in idx]
        for i in idx:
            for h in heads:
                st = state_ref[h]
                o_ref[crow[i], vsl[h]] = _dot(jnp.concatenate([att[i][h], q_in[i][:, ksl[h]]], axis=1),
                                              jnp.concatenate([v_ref[crow[i], vsl[h]], st.astype(BF16)], axis=0))
                state_ref[h] = st * jnp.concatenate([dec_t[i][h]] * (dv // n), axis=1) + contrib[i][h]


def _gla_fwd_kernel(q_ref, k_ref, v_ref, g_ref, w2_ref, b_ref, tri_ref, of_ref, state_ref, o_scr):
    _gla_scan(q_ref, k_ref, v_ref, g_ref, w2_ref, b_ref, tri_ref, o_scr, state_ref, False)
    of_ref[...] = o_scr[...].astype(BF16)


def _gla_bwd_kernel(q_ref, k_ref, v_ref, g_ref, w2_ref, b_ref, tri_ref,
                    of_ref, r_ref, yf_ref, h_ref, gn_ref, wf_ref, wg_ref, out_ref, state_ref, o_scr):
    _gla_scan(q_ref, k_ref, v_ref, g_ref, w2_ref, b_ref, tri_ref, o_scr, state_ref, True)
    ts = q_ref.shape[0]
    for r0 in range(0, ts, min(EPILOGUE_SUB, ts)):
        rs = slice(r0, r0 + min(EPILOGUE_SUB, ts))
        o = o_scr[rs, :] + of_ref[rs, :].astype(F32)
        normed = []
        for h in range(GLA_HEADS):
            oh = o[:, h * GLA_DV_PAD:(h + 1) * GLA_DV_PAD]
            ms = jnp.sum(oh * oh, axis=-1, keepdims=True) * (1.0 / GLA_DV)
            normed.append(oh * lax.rsqrt(ms + EPS))
        yg = jnp.concatenate(normed, axis=1) * gn_ref[...]
        yg = (yg * jax.nn.silu(r_ref[rs, :].astype(F32))).astype(BF16)
        out_ref[rs, :] = h_ref[rs, :] + _dot(yf_ref[rs, :].astype(BF16), wf_ref[...]) + _dot(yg, wg_ref[...])


def _gla(q, k, v, g, w2f, bf, w2b, bb, r, yf, h, gnorm, w_out_f, w_out_g, ts):
    bsz, s, _ = q.shape
    d = h.shape[-1]
    nblk = s // ts
    gw = 2 * GLA_GATE_RANK
    fwd = lambda b, i: (b, i, 0)
    bwd = lambda b, i: (b, nblk - 1 - i, 0)
    const = lambda b, i: (0, 0)

    def seq_specs(rev):
        im = bwd if rev else fwd
        return [pl.BlockSpec((None, ts, GLA_QK_PAD), im), pl.BlockSpec((None, ts, GLA_QK_PAD), im),
                pl.BlockSpec((None, ts, GLA_V_PAD), im), pl.BlockSpec((None, ts, gw), im),
                pl.BlockSpec((gw, GLA_QK_PAD), const), pl.BlockSpec((1, GLA_QK_PAD), const),
                pl.BlockSpec((GCHUNK, GCHUNK), const)]

    scratch = [pltpu.VMEM((GLA_HEADS, GLA_DK_PAD, GLA_DV_PAD), F32), pltpu.VMEM((ts, GLA_V_PAD), F32)]
    o_f = pl.pallas_call(
        _gla_fwd_kernel,
        grid=(bsz, nblk),
        in_specs=seq_specs(False),
        out_specs=pl.BlockSpec((None, ts, GLA_V_PAD), fwd),
        out_shape=jax.ShapeDtypeStruct((bsz, s, GLA_V_PAD), BF16),
        scratch_shapes=scratch,
        compiler_params=_cparams("parallel", "arbitrary"),
        name="gla_fwd",
    )(q, k, v, g, w2f, bf, _gla_cum_table(False))
    return pl.pallas_call(
        _gla_bwd_kernel,
        grid=(bsz, nblk),
        in_specs=seq_specs(True) + [
            pl.BlockSpec((None, ts, GLA_V_PAD), bwd), pl.BlockSpec((None, ts, GLA_V_PAD), bwd),
            pl.BlockSpec((None, ts, FNET_WIDTH), bwd), pl.BlockSpec((None, ts, d), bwd),
            pl.BlockSpec((1, GLA_V_PAD), const), pl.BlockSpec((FNET_WIDTH, d), const),
            pl.BlockSpec((GLA_V_PAD, d), const)],
        out_specs=pl.BlockSpec((None, ts, d), bwd),
        out_shape=jax.ShapeDtypeStruct((bsz, s, d), F32),
        scratch_shapes=scratch,
        compiler_params=_cparams("parallel", "arbitrary"),
        name="gla_bwd_out",
    )(q, k, v, g, w2b, bb, _gla_cum_table(True), o_f, r, yf, h, gnorm, w_out_f, w_out_g)


MCHUNK = 128
assert MCHUNK == MLSTM_DH
MGATE_ROWS = 2 * MLSTM_HEADS
ZROWS = 80
ZCOLS = 3 * MLSTM_DH


@functools.lru_cache(maxsize=None)
def _mlstm_zsel(reverse):
    z = np.zeros((ZROWS, MLSTM_HEADS * ZCOLS), np.float32)
    rbase = MLSTM_HEADS if reverse else 0
    for h in range(MLSTM_HEADS):
        for t in range(3):
            for k in range(3):
                z[(3 * t + k) * MGATE_ROWS + rbase + h, h * ZCOLS + t * MLSTM_DH:h * ZCOLS + (t + 1) * MLSTM_DH] = 1.0
    return z.astype(BF16)


def _cummax_lanes(x, reverse):
    n = x.shape[-1]
    lane = lax.broadcasted_iota(jnp.int32, x.shape, 1)
    s = 1
    while s < n:
        if reverse:
            shifted = jnp.where(lane < n - s, pltpu.roll(x, n - s, axis=1), -jnp.inf)
        else:
            shifted = jnp.where(lane >= s, pltpu.roll(x, s, axis=1), -jnp.inf)
        x = jnp.maximum(x, shifted)
        s *= 2
    return x


def _mlstm_scan(q_ref, k_ref, v_ref, gi_ref, gf_ref, zsel_ref, o_ref, st_ref, m_ref, reverse):
    ts = q_ref.shape[0]
    n = MCHUNK
    nck = ts // n
    dh = MLSTM_DH
    gr = MGATE_ROWS
    mask, tri_t_bf = _tri_masks(n, reverse)
    rbase = MLSTM_HEADS if reverse else 0

    @pl.when(pl.program_id(1) == 0)
    def _():
        st_ref[...] = jnp.zeros_like(st_ref)
        m_ref[...] = jnp.zeros_like(m_ref)

    gi = gi_ref[...].reshape(nck * gr, n) * LOG2E
    gf = gf_ref[...].reshape(nck * gr, n)
    cum = _cumsum_cols(_log2_sigmoid(gf), tri_t_bf)
    b = gi - cum
    md = cum + _cummax_lanes(b, reverse)
    tot = cum[:, 0:1] if reverse else cum[:, n - 1:n]
    w_end = tot + b
    m_loc = jnp.max(w_end, axis=1, keepdims=True)
    e = jnp.exp2(w_end - m_loc)
    ones = jnp.ones((n, dh), BF16)
    zpad = jnp.zeros((gr, n), BF16)
    zsel = zsel_ref[...]

    order = [nck - 1 - ci if reverse else ci for ci in range(nck)]
    heads = range(MLSTM_HEADS)
    grow = [slice(c * gr, (c + 1) * gr) for c in range(nck)]
    crow = [slice(c * n, (c + 1) * n) for c in range(nck)]
    hsl = [slice(h * dh, (h + 1) * dh) for h in heads]
    m_prev, fa, fb = [None] * nck, [None] * nck, [None] * nck
    m = m_ref[...]
    for c in order:
        m_prev[c] = m
        m_new = jnp.maximum(tot[grow[c]] + m, m_loc[grow[c]])
        fa[c] = jnp.exp2(tot[grow[c]] + m - m_new)
        fb[c] = jnp.exp2(m_loc[grow[c]] - m_new)
        m = m_new
    m_ref[...] = m
    z = [None] * nck
    for c in order:
        lw = cum[grow[c]] + m_prev[c][:, :n]
        m_t = jnp.maximum(lw, md[grow[c]])
        pieces = _split_bf16(cum[grow[c]] - m_t, 3) + _split_bf16(lw - m_t, 3) + _split_bf16(-m_t, 3) + [zpad]
        z[c] = _dot_tn(jnp.concatenate(pieces, axis=0), zsel)
    kt = [[k_ref[crow[c], hsl[h]].astype(F32).T for h in heads] for c in order]
    qk = [[_dot(q_ref[crow[c], hsl[h]], kt[i][h].astype(BF16)) for h in heads] for i, c in enumerate(order)]
    vaug = [[jnp.concatenate([v_ref[crow[c], hsl[h]], ones], axis=1) for h in heads] for c in order]
    s_qk, qa, floor, contrib = ([[None] * MLSTM_HEADS for _ in order] for _ in range(4))
    for i, c in enumerate(order):
        for h in heads:
            r = c * gr + rbase + h
            zb = h * ZCOLS
            expo = jnp.where(mask, z[c][:, zb:zb + dh] + b[r:r + 1, :], -jnp.inf)
            s_qk[i][h] = (qk[i][h] * jnp.exp2(expo)).astype(BF16)
            qa[i][h] = (q_ref[crow[c], hsl[h]].astype(F32) * jnp.exp2(z[c][:, zb + dh:zb + 2 * dh])).astype(BF16)
            floor[i][h] = jnp.exp2(z[c][:, zb + 2 * dh:zb + 3 * dh])
            ket = (kt[i][h] * e[r:r + 1, :]).astype(BF16)
            contrib[i][h] = _dot(ket, vaug[i][h])
    for i, c in enumerate(order):
        for h in heads:
            rr = rbase + h
            st = st_ref[h]
            res = _dot(jnp.concatenate([s_qk[i][h], qa[i][h]], axis=1),
                       jnp.concatenate([vaug[i][h], st.astype(BF16)], axis=0))
            o_ref[crow[c], hsl[h]] = res[:, :dh] / jnp.maximum(jnp.abs(res[:, dh:]), floor[i][h])
            st_ref[h] = fa[c][rr:rr + 1, :] * st + fb[c][rr:rr + 1, :] * contrib[i][h]


def _mlstm_fwd_kernel(q_ref, k_ref, v_ref, gi_ref, gf_ref, zsel_ref, hf_ref, st_ref, m_ref, o_scr):
    _mlstm_scan(q_ref, k_ref, v_ref, gi_ref, gf_ref, zsel_ref, o_scr, st_ref, m_ref, False)
    hf_ref[...] = o_scr[...].astype(BF16)


def _mlstm_bwd_kernel(q_ref, k_ref, v_ref, gi_ref, gf_ref, zsel_ref, hf_ref, og_ref, sb_ref, t_ref, tp_ref, tn_ref,
                      h_ref, mn_ref, cw_ref, cb_ref, wc_ref, wm_ref, out_ref, st_ref, m_ref, o_scr):
    _mlstm_scan(q_ref, k_ref, v_ref, gi_ref, gf_ref, zsel_ref, o_scr, st_ref, m_ref, True)
    ts = q_ref.shape[0]
    i = pl.program_id(1)
    nblk = pl.num_programs(1)
    prev = jnp.where(i < nblk - 1, tp_ref[...].astype(F32), 0.0)
    nxt = jnp.where(i > 0, tn_ref[...].astype(F32), 0.0)
    ext = jnp.concatenate([prev, t_ref[...].astype(F32), nxt], axis=0)
    rows = ts + 2 * HALO_BF16
    lo = pltpu.roll(ext, 1, axis=0)[HALO_BF16:HALO_BF16 + ts]
    hi = pltpu.roll(ext, rows - 1, axis=0)[HALO_BF16:HALO_BF16 + ts]
    cw = cw_ref[...]
    conv = lo * cw[0:1, :] + ext[HALO_BF16:HALO_BF16 + ts] * cw[1:2, :] + hi * cw[2:3, :] + cb_ref[...]
    yc = (sb_ref[...].astype(F32) * conv).astype(BF16)
    for r0 in range(0, ts, min(EPILOGUE_SUB, ts)):
        rs = slice(r0, r0 + min(EPILOGUE_SUB, ts))
        o = o_scr[rs, :] + hf_ref[rs, :].astype(F32)
        normed = []
        for h in range(MLSTM_HEADS):
            oh = o[:, h * MLSTM_DH:(h + 1) * MLSTM_DH]
            normed.append(oh * lax.rsqrt(jnp.mean(oh * oh, axis=-1, keepdims=True) + EPS))
        ym = jnp.concatenate(normed, axis=1) * mn_ref[...]
        ym = (ym * jax.nn.sigmoid(og_ref[rs, :].astype(F32))).astype(BF16)
        out_ref[rs, :] = h_ref[rs, :] + _dot(yc[rs, :], wc_ref[...]) + _dot(ym, wm_ref[...])


def _mlstm(q, k, v, gi, gf, og, sb, t, h, mnorm, conv_w, conv_b, w_out_c, w_out_m, ts):
    bsz, s, _ = q.shape
    d = h.shape[-1]
    nblk = s // ts
    nck = ts // MCHUNK
    hb = ts // HALO_BF16
    fwd = lambda b, i: (b, i, 0)
    bwd = lambda b, i: (b, nblk - 1 - i, 0)
    const = lambda b, i: (0, 0)

    def seq_specs(rev):
        w = MLSTM_WIDTH
        blk = (lambda i: nblk - 1 - i) if rev else (lambda i: i)
        im = lambda b, i: (b, blk(i), 0)
        im4 = lambda b, i: (b, blk(i), 0, 0)
        gspec = pl.BlockSpec((None, nck, MGATE_ROWS, MCHUNK), im4)
        return [pl.BlockSpec((None, ts, w), im), pl.BlockSpec((None, ts, w), im), pl.BlockSpec((None, ts, w), im),
                gspec, gspec, pl.BlockSpec((ZROWS, MLSTM_HEADS * ZCOLS), const)]

    scratch = [pltpu.VMEM((MLSTM_HEADS, MLSTM_DH, 2 * MLSTM_DH), F32), pltpu.VMEM((MGATE_ROWS, 2 * MLSTM_DH), F32),
               pltpu.VMEM((ts, MLSTM_WIDTH), F32)]
    h_f = pl.pallas_call(
        _mlstm_fwd_kernel,
        grid=(bsz, nblk),
        in_specs=seq_specs(False),
        out_specs=pl.BlockSpec((None, ts, MLSTM_WIDTH), fwd),
        out_shape=jax.ShapeDtypeStruct((bsz, s, MLSTM_WIDTH), BF16),
        scratch_shapes=scratch,
        compiler_params=_cparams("parallel", "arbitrary"),
        name="mlstm_fwd",
    )(q, k, v, gi, gf, _mlstm_zsel(False))
    last16 = s // HALO_BF16 - 1
    halo_prev = lambda b, i: (b, jnp.maximum((nblk - 1 - i) * hb - 1, 0), 0)
    halo_next = lambda b, i: (b, jnp.minimum((nblk - i) * hb, last16), 0)
    return pl.pallas_call(
        _mlstm_bwd_kernel,
        grid=(bsz, nblk),
        in_specs=seq_specs(True) + [
            pl.BlockSpec((None, ts, MLSTM_WIDTH), bwd), pl.BlockSpec((None, ts, MLSTM_WIDTH), bwd),
            pl.BlockSpec((None, ts, CONV_WIDTH), bwd), pl.BlockSpec((None, ts, CONV_WIDTH), bwd),
            pl.BlockSpec((None, HALO_BF16, CONV_WIDTH), halo_prev),
            pl.BlockSpec((None, HALO_BF16, CONV_WIDTH), halo_next),
            pl.BlockSpec((None, ts, d), bwd),
            pl.BlockSpec((1, MLSTM_WIDTH), const), pl.BlockSpec((3, CONV_WIDTH), const),
            pl.BlockSpec((1, CONV_WIDTH), const), pl.BlockSpec((CONV_WIDTH, d), const),
            pl.BlockSpec((MLSTM_WIDTH, d), const)],
        out_specs=pl.BlockSpec((None, ts, d), bwd),
        out_shape=jax.ShapeDtypeStruct((bsz, s, d), F32),
        scratch_shapes=scratch,
        compiler_params=_cparams("parallel", "arbitrary"),
        name="mlstm_bwd_out",
    )(q, k, v, gi, gf, _mlstm_zsel(True), h_f, og, sb, t, t, t, h, mnorm, conv_w, conv_b, w_out_c, w_out_m)


FFN_SUB = ROW_TILE


def _ffn_kernel(h_ref, hp_ref, hn_ref, p_ref, fg_ref, wu_ref, cw_ref, cb_ref, wd_ref, pg_ref, pgw_ref, pw_ref,
                fn_ref, out_ref, xn_scr, act_scr, *, d_ff, f_tile, final):
    ts = h_ref.shape[0]
    i = pl.program_id(1)
    nblk = pl.num_programs(1)
    fg = fg_ref[...]
    sub = min(FFN_SUB, ts)
    rows = sub + 2 * HALO
    for k, r0 in enumerate(range(0, ts, sub)):
        x = h_ref[r0:r0 + sub, :]
        if r0 == 0:
            prev = jnp.where(i > 0, _rms(hp_ref[...], fg), 0.0)
        else:
            prev = _rms(h_ref[r0 - HALO:r0, :], fg)
        if r0 + sub == ts:
            nxt = jnp.where(i < nblk - 1, _rms(hn_ref[...], fg), 0.0)
        else:
            nxt = _rms(h_ref[r0 + sub:r0 + sub + HALO, :], fg)
        xn_scr[k, 0:HALO, :] = prev.astype(BF16)
        xn_scr[k, HALO:HALO + sub, :] = _rms(x, fg).astype(BF16)
        xn_scr[k, HALO + sub:, :] = nxt.astype(BF16)
        xe = xn_scr[k]
        xc = xe[HALO:HALO + sub]
        for f0 in range(0, d_ff, f_tile):
            gate = _dot(xe, wu_ref[:, f0:f0 + f_tile])
            val = _dot(xc, wu_ref[:, d_ff + f0:d_ff + f0 + f_tile])
            cw = cw_ref[:, f0:f0 + f_tile]
            lo = pltpu.roll(gate, 1, axis=0)[HALO:HALO + sub]
            hi = pltpu.roll(gate, rows - 1, axis=0)[HALO:HALO + sub]
            conv = (lo * cw[0:1, :] + gate[HALO:HALO + sub] * cw[1:2, :] + hi * cw[2:3, :]
                    + cb_ref[:, f0:f0 + f_tile])
            act_scr[r0:r0 + sub, f0:f0 + f_tile] = (jax.nn.silu(conv) * val).astype(BF16)
        h2 = x + _dot(act_scr[r0:r0 + sub, :], wd_ref[...])
        gate = jax.nn.sigmoid(_dot(_rms(h2, pg_ref[...]).astype(BF16), pgw_ref[...]))
        h3 = h2 + _dot(p_ref[r0:r0 + sub, :].astype(BF16), pw_ref[...]) * gate
        out_ref[r0:r0 + sub, :] = _rms(h3, fn_ref[...]) if final else h3


def _ffn(h, p, layer, fgain, w_up, conv_w, conv_b, w_down, pgain, pg_w, p_w, final_gain, final, ts, f_tile):
    bsz, s, d = h.shape
    d_ff = w_down.shape[0]
    nblk = s // ts
    hb = ts // HALO
    last8 = s // HALO - 1
    blk = lambda b, i: (b, i, 0)
    const = lambda b, i: (0, 0)
    once = pl.Buffered(1)

    def wspec(shape):
        return pl.BlockSpec(shape, const, pipeline_mode=once)

    return pl.pallas_call(
        functools.partial(_ffn_kernel, d_ff=d_ff, f_tile=f_tile, final=final),
        grid=(bsz, nblk),
        in_specs=[pl.BlockSpec((None, ts, d), blk),
                  pl.BlockSpec((None, HALO, d), lambda b, i: (b, jnp.maximum(i * hb - 1, 0), 0)),
                  pl.BlockSpec((None, HALO, d), lambda b, i: (b, jnp.minimum((i + 1) * hb, last8), 0)),
                  pl.BlockSpec((None, None, ts, p.shape[-1]), lambda b, i: (layer, b, i, 0)),
                  wspec((1, d)), wspec((d, 2 * d_ff)), wspec((3, d_ff)), wspec((1, d_ff)), wspec((d_ff, d)),
                  wspec((1, d)), wspec((d, d)), wspec((p.shape[-1], d)), wspec((1, d))],
        out_specs=pl.BlockSpec((None, ts, d), blk),
        out_shape=jax.ShapeDtypeStruct((bsz, s, d), F32),
        scratch_shapes=[pltpu.VMEM((-(-ts // FFN_SUB), min(FFN_SUB, ts) + 2 * HALO, d), BF16),
                        pltpu.VMEM((ts, d_ff), BF16)],
        compiler_params=_cparams("parallel", "parallel"),
        name="ffn_ple",
    )(h, h, h, p, fgain, w_up, conv_w, conv_b, w_down, pgain, pg_w, p_w, final_gain)


def _pad_heads_cols(w, heads, dh, dpad):
    lead = w.shape[:-1]
    w = w.reshape(lead + (heads, dh))
    w = jnp.pad(w, [(0, 0)] * len(lead) + [(0, 0), (0, dpad - dh)])
    return w.reshape(lead + (heads * dpad,))


def _pad_cols(w, width):
    return jnp.pad(w, [(0, 0)] * (w.ndim - 1) + [(0, width - w.shape[-1])])


def _even_weights(w_in, w2_f, b_f, w2_b, b_b, gnorm, w_out):
    cuts = np.cumsum([FNET_WIDTH, GLA_HEADS * GLA_DK, GLA_HEADS * GLA_DK, GLA_HEADS * GLA_DV,
                      GLA_HEADS * GLA_DV]).tolist()
    wu, wq, wk, wv, wr, wg = jnp.split(w_in, cuts, axis=-1)
    w_pad = jnp.concatenate([
        wu, _pad_heads_cols(wq * (GLA_DK ** -0.5), GLA_HEADS, GLA_DK, GLA_DK_PAD),
        _pad_heads_cols(wk, GLA_HEADS, GLA_DK, GLA_DK_PAD),
        _pad_heads_cols(wv, GLA_HEADS, GLA_DV, GLA_DV_PAD), _pad_heads_cols(wr, GLA_HEADS, GLA_DV, GLA_DV_PAD),
        _pad_cols(wg, GATE_PAD)], axis=-1).astype(BF16)
    zero = jnp.zeros((GLA_GATE_RANK, GLA_QK_PAD), F32)
    w2f = jnp.concatenate([_pad_heads_cols(w2_f, GLA_HEADS, GLA_DK, GLA_DK_PAD), zero], axis=0)
    w2b = jnp.concatenate([zero, _pad_heads_cols(w2_b, GLA_HEADS, GLA_DK, GLA_DK_PAD)], axis=0)
    bf = _pad_heads_cols(b_f[None, :], GLA_HEADS, GLA_DK, GLA_DK_PAD)
    bb = _pad_heads_cols(b_b[None, :], GLA_HEADS, GLA_DK, GLA_DK_PAD)
    gn = _pad_heads_cols(gnorm[None, :], GLA_HEADS, GLA_DV, GLA_DV_PAD)
    w_out_f = w_out[:FNET_WIDTH].astype(BF16)
    wog = w_out[FNET_WIDTH:].reshape(GLA_HEADS, GLA_DV, -1)
    wog = jnp.pad(wog, ((0, 0), (0, GLA_DV_PAD - GLA_DV), (0, 0))).reshape(GLA_V_PAD, -1).astype(BF16)
    return w_pad, w2f, bf, w2b, bb, gn, w_out_f, wog


def _trunk(x, p, e_norm, e_w_in, e_gla_w2_f, e_gla_b_f, e_gla_w2_b, e_gla_b_b, e_gla_norm, e_w_out,
           o_norm, o_w_in, o_conv_w, o_conv_b, o_gate_bias, o_mlstm_norm, o_w_out,
           ffn_norm, ffn_w_up, ffn_conv_w, ffn_conv_b, ffn_w_down,
           ple_w, ple_gate_norm, ple_gate_w, final_norm, *, ts=ROW_TILE, f_tile=FFN_TILE):
    bsz, s, d = x.shape
    depth = p.shape[0]
    t = bsz * s
    ts = min(ts, s)
    h = x
    for layer in range(depth):
        j = layer // 2
        h2d = h.reshape(t, d)
        if layer % 2 == 0:
            w_pad, w2f, bf, w2b, bb, gn, w_out_f, w_out_g = _even_weights(
                e_w_in[j], e_gla_w2_f[j], e_gla_b_f[j], e_gla_w2_b[j], e_gla_b_b[j], e_gla_norm[j], e_w_out[j])
            a, b, q, k, v, r, g = _inproj_even(h2d, e_norm[j][None, :], w_pad, _channel_dft(), ts)
            sh = lambda z: z.reshape(bsz, s, z.shape[-1])
            yf = _fnet(sh(a), sh(b))
            h = _gla(sh(q), sh(k), sh(v), sh(g), w2f, bf, w2b, bb, sh(r), yf, h, gn, w_out_f, w_out_g, ts)
        else:
            q_lo = 3 * CONV_WIDTH
            col = jnp.arange(o_w_in.shape[-1])
            qscale = jnp.where((col >= q_lo) & (col < q_lo + MLSTM_WIDTH), MLSTM_DH ** -0.5, 1.0)
            w_pad = _pad_cols(o_w_in[j] * qscale, ODD_W).astype(BF16)
            sb, tt, q, k, v, og, gates = _inproj_odd(h2d, o_norm[j][None, :], w_pad, o_gate_bias[j][None, :], ts)
            sh = lambda z: z.reshape(bsz, s, z.shape[-1])
            nh = MLSTM_HEADS
            gates = gates.reshape(bsz, s // MCHUNK, MCHUNK, ODD_GATES).transpose(0, 1, 3, 2)
            gi = jnp.concatenate([gates[:, :, 0:nh], gates[:, :, 2 * nh:3 * nh]], axis=2)
            gf = jnp.concatenate([gates[:, :, nh:2 * nh], gates[:, :, 3 * nh:4 * nh]], axis=2)
            w_out = o_w_out[j].astype(BF16)
            h = _mlstm(sh(q), sh(k), sh(v), gi, gf, sh(og), sh(sb), sh(tt), h,
                       o_mlstm_norm[j][None, :], o_conv_w[j], o_conv_b[j][None, :],
                       w_out[:CONV_WIDTH], w_out[CONV_WIDTH:], ts)
        h = _ffn(h, p, layer, ffn_norm[layer][None, :], ffn_w_up[layer].astype(BF16), ffn_conv_w[layer],
                 ffn_conv_b[layer][None, :], ffn_w_down[layer].astype(BF16), ple_gate_norm[layer][None, :],
                 ple_gate_w[layer].astype(BF16), ple_w[layer].astype(BF16), final_norm[None, :],
                 layer == depth - 1, ts, f_tile)
    return h


def kernel(x_prompt, x_sample, p_prompt, p_sample, e_norm, e_w_in, e_gla_w2_f, e_gla_b_f, e_gla_w2_b, e_gla_b_b, e_gla_norm, e_w_out, o_norm, o_w_in, o_conv_w, o_conv_b, o_gate_bias, o_mlstm_norm, o_w_out, ffn_norm, ffn_w_up, ffn_conv_w, ffn_conv_b, ffn_w_down, ple_w, ple_gate_norm, ple_gate_w, final_norm):
    weights = (e_norm, e_w_in, e_gla_w2_f, e_gla_b_f, e_gla_w2_b, e_gla_b_b, e_gla_norm, e_w_out,
               o_norm, o_w_in, o_conv_w, o_conv_b, o_gate_bias, o_mlstm_norm, o_w_out,
               ffn_norm, ffn_w_up, ffn_conv_w, ffn_conv_b, ffn_w_down,
               ple_w, ple_gate_norm, ple_gate_w, final_norm)
    return (_trunk(x_prompt, p_prompt, *weights), _trunk(x_sample, p_sample, *weights))
```

```python
import functools
import math

import numpy as np
import jax
import jax.numpy as jnp
from jax import lax
from jax.experimental import pallas as pl
from jax.experimental.pallas import tpu as pltpu

F32 = jnp.float32
BF16 = jnp.bfloat16
EPS = 1e-6

FNET_GROUPS = 4
FNET_GROUP_DIM = 64
FNET_WIDTH = FNET_GROUPS * FNET_GROUP_DIM
GLA_HEADS = 4
GLA_DK = 96
GLA_DV = 192
GLA_GATE_RANK = 16
GLA_TAU = 16.0
CONV_WIDTH = 512
MLSTM_HEADS = 4
MLSTM_DH = 128
MLSTM_WIDTH = MLSTM_HEADS * MLSTM_DH

GLA_DK_PAD = 128
GLA_DV_PAD = 256
GLA_QK_PAD = GLA_HEADS * GLA_DK_PAD
GLA_V_PAD = GLA_HEADS * GLA_DV_PAD
GATE_PAD = 128

V7X_VMEM_BYTES = 64 * 1024 * 1024
VMEM_LIMIT = V7X_VMEM_BYTES - 8 * 1024 * 1024

ROW_TILE = 1024
FFN_TILE = 256
HALO = 8
HALO_BF16 = 16


def _cparams(*sem):
    return pltpu.CompilerParams(dimension_semantics=sem, vmem_limit_bytes=VMEM_LIMIT)


def _rms(x, g):
    ms = jnp.mean(x * x, axis=-1, keepdims=True)
    return x * lax.rsqrt(ms + EPS) * g


def _split_bf16(x, n):
    parts = []
    r = x
    for _ in range(n):
        p = r.astype(BF16)
        parts.append(p)
        r = r - p.astype(F32)
    return parts


def _dot(a, b):
    return jnp.dot(a, b, preferred_element_type=F32)


LOG2E = 1.4426950408889634


def _log2_sigmoid(x):
    return jnp.minimum(x, 0.0) * LOG2E - jnp.log2(1.0 + jnp.exp2(jnp.abs(x) * (-LOG2E)))


def _tri_masks(n, reverse):
    r = lax.broadcasted_iota(jnp.int32, (n, n), 0)
    c = lax.broadcasted_iota(jnp.int32, (n, n), 1)
    mask = (c >= r) if reverse else (c <= r)
    return mask, ((r >= c) if reverse else (r <= c)).astype(BF16)


def _cumsum_cols(x, tri_t_bf):
    acc = None
    for p in _split_bf16(x, 3):
        t = _dot(p, tri_t_bf)
        acc = t if acc is None else acc + t
    return acc


INPROJ_SUB = 512
EVEN_SEGS = (FNET_WIDTH, GLA_QK_PAD, GLA_QK_PAD, GLA_V_PAD, GLA_V_PAD, GATE_PAD)
EVEN_W = sum(EVEN_SEGS)


def _inproj_even_kernel(h_ref, g_ref, w_ref, dft_ref, a_ref, b_ref, q_ref, k_ref, v_ref, r_ref, gg_ref):
    offs = np.cumsum((0,) + EVEN_SEGS)
    tm = h_ref.shape[0]
    for r0 in range(0, tm, INPROJ_SUB):
        rows = slice(r0, r0 + INPROJ_SUB)
        xn = _rms(h_ref[rows, :], g_ref[...]).astype(BF16)

        def seg(i):
            return _dot(xn, w_ref[:, offs[i]:offs[i + 1]])

        u = seg(0).astype(BF16)
        ab = _dot(u, dft_ref[...])
        a_ref[rows, :] = ab[:, :FNET_WIDTH]
        b_ref[rows, :] = ab[:, FNET_WIDTH:]
        q_ref[rows, :] = seg(1).astype(BF16)
        k_ref[rows, :] = seg(2).astype(BF16)
        v_ref[rows, :] = seg(3).astype(BF16)
        r_ref[rows, :] = seg(4).astype(BF16)
        gg_ref[rows, :] = seg(5)[:, :2 * GLA_GATE_RANK]


def _inproj_even(h2d, gain, w_pad, dft_cs, tm):
    t, d = h2d.shape
    row = lambda i: (i, 0)
    const = lambda i: (0, 0)
    widths = (FNET_WIDTH, FNET_WIDTH, GLA_QK_PAD, GLA_QK_PAD, GLA_V_PAD, GLA_V_PAD, 2 * GLA_GATE_RANK)
    dts = (F32, F32) + (BF16,) * 4 + (F32,)
    return pl.pallas_call(
        _inproj_even_kernel,
        grid=(t // tm,),
        in_specs=[pl.BlockSpec((tm, d), row), pl.BlockSpec((1, d), const),
                  pl.BlockSpec((d, EVEN_W), const), pl.BlockSpec((FNET_WIDTH, 2 * FNET_WIDTH), const)],
        out_specs=[pl.BlockSpec((tm, w), row) for w in widths],
        out_shape=[jax.ShapeDtypeStruct((t, w), dt) for w, dt in zip(widths, dts)],
        compiler_params=_cparams("parallel"),
        name="inproj_even",
    )(h2d, gain, w_pad, dft_cs)


ODD_GATES = 4 * MLSTM_HEADS
ODD_W = 3 * CONV_WIDTH + 4 * MLSTM_WIDTH + GATE_PAD


def _inproj_odd_kernel(h_ref, g_ref, w_ref, gb_ref, sb_ref, t_ref, q_ref, k_ref, v_ref, og_ref, gt_ref):
    cw = CONV_WIDTH
    base = 3 * cw
    tm = h_ref.shape[0]
    for r0 in range(0, tm, INPROJ_SUB):
        rows = slice(r0, r0 + INPROJ_SUB)
        xn = _rms(h_ref[rows, :], g_ref[...]).astype(BF16)

        def seg(lo, n):
            return _dot(xn, w_ref[:, lo:lo + n])

        sb_ref[rows, :] = seg(0, cw).astype(BF16)
        t_ref[rows, :] = (seg(cw, cw) * seg(2 * cw, cw)).astype(BF16)
        q_ref[rows, :] = seg(base, MLSTM_WIDTH).astype(BF16)
        k_ref[rows, :] = seg(base + MLSTM_WIDTH, MLSTM_WIDTH).astype(BF16)
        v_ref[rows, :] = seg(base + 2 * MLSTM_WIDTH, MLSTM_WIDTH).astype(BF16)
        og_ref[rows, :] = seg(base + 3 * MLSTM_WIDTH, MLSTM_WIDTH).astype(BF16)
        gt_ref[rows, :] = seg(base + 4 * MLSTM_WIDTH, GATE_PAD)[:, :ODD_GATES] + gb_ref[...]


def _inproj_odd(h2d, gain, w_pad, gate_bias, tm):
    t, d = h2d.shape
    row = lambda i: (i, 0)
    const = lambda i: (0, 0)
    widths = (CONV_WIDTH, CONV_WIDTH) + (MLSTM_WIDTH,) * 4 + (ODD_GATES,)
    dts = (BF16,) * 6 + (F32,)
    return pl.pallas_call(
        _inproj_odd_kernel,
        grid=(t // tm,),
        in_specs=[pl.BlockSpec((tm, d), row), pl.BlockSpec((1, d), const),
                  pl.BlockSpec((d, ODD_W), const), pl.BlockSpec((1, ODD_GATES), const)],
        out_specs=[pl.BlockSpec((tm, w), row) for w in widths],
        out_shape=[jax.ShapeDtypeStruct((t, w), dt) for w, dt in zip(widths, dts)],
        compiler_params=_cparams("parallel"),
        name="inproj_odd",
    )(h2d, gain, w_pad, gate_bias)


def _fnet_factors(s):
    lg = int(round(math.log2(s)))
    assert 2 ** lg == s
    n1 = 2 ** ((lg + 1) // 2)
    return n1, s // n1


@functools.lru_cache(maxsize=None)
def _fnet_tables(s):
    n1, n2 = _fnet_factors(s)
    k1 = np.arange(n1, dtype=np.int64)[None, :, None]
    m1 = np.arange(n1, dtype=np.int64)[None, None, :]
    j2 = np.arange(n2, dtype=np.int64)[:, None, None]
    ang = 2.0 * np.pi * ((k1 * (j2 + n2 * m1)) % s).astype(np.float64) / s
    gc = np.cos(ang) / np.sqrt(n1)
    gs = np.sin(ang) / np.sqrt(n1)
    g = np.concatenate([np.concatenate([gc, -gs], axis=2), np.concatenate([-gs, -gc], axis=2)], axis=1)
    k2 = np.arange(n2, dtype=np.int64)[:, None]
    m2 = np.arange(n2, dtype=np.int64)[None, :]
    ang2 = 2.0 * np.pi * ((k2 * m2) % n2).astype(np.float64) / n2
    f2 = np.concatenate([np.cos(ang2), np.sin(ang2)], axis=1) / np.sqrt(n2)
    return g.astype(np.float32).astype(BF16), f2.astype(np.float32).astype(BF16)


@functools.lru_cache(maxsize=None)
def _channel_dft():
    j = np.arange(FNET_WIDTH)
    same = (j[:, None] // FNET_GROUP_DIM) == (j[None, :] // FNET_GROUP_DIM)
    ang = 2.0 * np.pi * (((j[:, None] % FNET_GROUP_DIM) * (j[None, :] % FNET_GROUP_DIM)) % FNET_GROUP_DIM) / FNET_GROUP_DIM
    c = np.where(same, np.cos(ang), 0.0) / np.sqrt(FNET_GROUP_DIM)
    sn = np.where(same, np.sin(ang), 0.0) / np.sqrt(FNET_GROUP_DIM)
    return np.concatenate([c, sn], axis=1).astype(np.float32).astype(BF16)


FNET_T = 16


def _fnet1_kernel(a_ref, b_ref, g_ref, p_ref, q_ref, *, n1):
    for t in range(FNET_T):
        rhs = jnp.concatenate([a_ref[:, t, :], b_ref[:, t, :]], axis=0).astype(BF16)
        res = _dot(g_ref[t], rhs)
        p_ref[:, t, :] = res[:n1]
        q_ref[:, t, :] = res[n1:]


def _fnet2_kernel(p_ref, q_ref, f_ref, y_ref, *, n2):
    for t in range(FNET_T):
        rs = slice(t * n2, (t + 1) * n2)
        rhs = jnp.concatenate([p_ref[rs, :], q_ref[rs, :]], axis=0).astype(BF16)
        y_ref[:, t, :] = _dot(f_ref[...], rhs)


def _fnet(a, b):
    bsz, s, c = a.shape
    n1, n2 = _fnet_factors(s)
    g_tab, f2_tab = _fnet_tables(s)
    t = FNET_T
    assert n1 % t == 0 and n2 % t == 0
    blk = pl.BlockSpec((None, n1, t, c), lambda j, i: (i, 0, j, 0))
    p, q = pl.pallas_call(
        functools.partial(_fnet1_kernel, n1=n1),
        grid=(n2 // t, bsz),
        in_specs=[blk, blk, pl.BlockSpec((t, 2 * n1, 2 * n1), lambda j, i: (j, 0, 0))],
        out_specs=[blk, blk],
        out_shape=[jax.ShapeDtypeStruct((bsz, n1, n2, c), F32)] * 2,
        compiler_params=_cparams("parallel", "parallel"),
        name="fnet_stage1",
    )(a.reshape(bsz, n1, n2, c), b.reshape(bsz, n1, n2, c), g_tab)
    rblk = pl.BlockSpec((None, t * n2, c), lambda i, j: (i, j, 0))
    y = pl.pallas_call(
        functools.partial(_fnet2_kernel, n2=n2),
        grid=(bsz, n1 // t),
        in_specs=[rblk, rblk, pl.BlockSpec((n2, 2 * n2), lambda i, j: (0, 0))],
        out_specs=pl.BlockSpec((None, n2, t, c), lambda i, j: (i, 0, j, 0)),
        out_shape=jax.ShapeDtypeStruct((bsz, n2, n1, c), F32),
        compiler_params=_cparams("parallel", "parallel"),
        name="fnet_stage2",
    )(p.reshape(bsz, s, c), q.reshape(bsz, s, c), f2_tab)
    return y.reshape(bsz, s, c)


GCHUNK = 128
EPILOGUE_SUB = 256
SCAN_GROUP = 4
GHALF = GCHUNK // 2


@functools.lru_cache(maxsize=None)
def _gla_cum_table(reverse):
    n = GCHUNK
    i = np.arange(n)[:, None]
    t = np.arange(n)[None, :]
    tri = (t >= i) if reverse else (t <= i)
    return (tri.astype(np.float32) / GLA_TAU).astype(BF16)


def _dot_split2(a, b):
    p = _split_bf16(b, 2)
    return _dot(a, p[0]) + _dot(a, p[1])


def _gla_scan(q_ref, k_ref, v_ref, g_ref, w2_ref, b_ref, tri_ref, o_ref, state_ref, reverse):
    ts = q_ref.shape[0]
    n = GCHUNK
    nck = ts // n
    dk, dv = GLA_DK_PAD, GLA_DV_PAD
    r = lax.broadcasted_iota(jnp.int32, (n, n), 0)
    c = lax.broadcasted_iota(jnp.int32, (n, n), 1)
    mask = (c >= r) if reverse else (c <= r)
    w2 = w2_ref[...].astype(BF16)
    tri = tri_ref[...]
    mid_row = GHALF if reverse else GHALF - 1
    end_row = 0 if reverse else n - 1

    @pl.when(pl.program_id(1) == 0)
    def _():
        state_ref[...] = jnp.zeros_like(state_ref)

    heads = range(GLA_HEADS)
    ksl = [slice(h * dk, (h + 1) * dk) for h in heads]
    vsl = [slice(h * dv, (h + 1) * dv) for h in heads]
    order = [nck - 1 - ci if reverse else ci for ci in range(nck)]
    for g0 in range(0, nck, SCAN_GROUP):
        grp = order[g0:g0 + SCAN_GROUP]
        idx = range(len(grp))
        crow = [slice(c * n, (c + 1) * n) for c in grp]
        la = [_log2_sigmoid(_dot(g_ref[crow[i], :].astype(BF16), w2) + b_ref[...]) for i in idx]
        cum = [_dot_split2(tri, la[i]) for i in idx]
        rel = [cum[i] - cum[i][mid_row:mid_row + 1, :] for i in idx]
        tot = [cum[i][end_row:end_row + 1, :] for i in idx]
        qf = [q_ref[crow[i], :].astype(F32) for i in idx]
        kf = [k_ref[crow[i], :].astype(F32) for i in idx]
        q_mid = [(qf[i] * jnp.exp2(rel[i])).astype(BF16) for i in idx]
        q_in = [(qf[i] * jnp.exp2(cum[i])).astype(BF16) for i in idx]
        k_mid = [kf[i] * jnp.exp2(-rel[i]) for i in idx]
        k_out = [kf[i] * jnp.exp2(tot[i] - cum[i]) for i in idx]
        dec = [jnp.broadcast_to(jnp.exp2(tot[i]), (n, GLA_QK_PAD)) for i in idx]
        k_mid_t = [[k_mid[i][:, ksl[h]].T.astype(BF16) for h in heads] for i in idx]
        k_out_t = [[k_out[i][:, ksl[h]].T.astype(BF16) for h in heads] for i in idx]
        dec_t = [[dec[i][:, ksl[h]].T for h in heads] for i in idx]
        att = [[jnp.where(mask, _dot(q_mid[i][:, ksl[h]], k_mid_t[i][h]), 0.0).astype(BF16) for h in heads]
               for i in idx]
        contrib = [[_dot(k_out_t[i][h], v_ref[crow[i], vsl[h]]) for h in heads] for i in idx]
        for i in idx:
            for h in heads:
                st = state_ref[h]
                o_ref[crow[i], vsl[h]] = _dot(jnp.concatenate([att[i][h], q_in[i][:, ksl[h]]], axis=1),
                                              jnp.concatenate([v_ref[crow[i], vsl[h]], st.astype(BF16)], axis=0))
                state_ref[h] = st * jnp.concatenate([dec_t[i][h]] * (dv // n), axis=1) + contrib[i][h]


def _gla_fwd_kernel(q_ref, k_ref, v_ref, g_ref, w2_ref, b_ref, tri_ref, of_ref, state_ref, o_scr):
    _gla_scan(q_ref, k_ref, v_ref, g_ref, w2_ref, b_ref, tri_ref, o_scr, state_ref, False)
    of_ref[...] = o_scr[...].astype(BF16)


def _gla_bwd_kernel(q_ref, k_ref, v_ref, g_ref, w2_ref, b_ref, tri_ref,
                    of_ref, r_ref, yf_ref, h_ref, gn_ref, wf_ref, wg_ref, out_ref, state_ref, o_scr):
    _gla_scan(q_ref, k_ref, v_ref, g_ref, w2_ref, b_ref, tri_ref, o_scr, state_ref, True)
    ts = q_ref.shape[0]
    for r0 in range(0, ts, min(EPILOGUE_SUB, ts)):
        rs = slice(r0, r0 + min(EPILOGUE_SUB, ts))
        o = o_scr[rs, :] + of_ref[rs, :].astype(F32)
        normed = []
        for h in range(GLA_HEADS):
            oh = o[:, h * GLA_DV_PAD:(h + 1) * GLA_DV_PAD]
            ms = jnp.sum(oh * oh, axis=-1, keepdims=True) * (1.0 / GLA_DV)
            normed.append(oh * lax.rsqrt(ms + EPS))
        yg = jnp.concatenate(normed, axis=1) * gn_ref[...]
        yg = (yg * jax.nn.silu(r_ref[rs, :].astype(F32))).astype(BF16)
        out_ref[rs, :] = h_ref[rs, :] + _dot(yf_ref[rs, :].astype(BF16), wf_ref[...]) + _dot(yg, wg_ref[...])


def _gla(q, k, v, g, w2f, bf, w2b, bb, r, yf, h, gnorm, w_out_f, w_out_g, ts):
    bsz, s, _ = q.shape
    d = h.shape[-1]
    nblk = s // ts
    gw = 2 * GLA_GATE_RANK
    fwd = lambda b, i: (b, i, 0)
    bwd = lambda b, i: (b, nblk - 1 - i, 0)
    const = lambda b, i: (0, 0)

    def seq_specs(rev):
        im = bwd if rev else fwd
        return [pl.BlockSpec((None, ts, GLA_QK_PAD), im), pl.BlockSpec((None, ts, GLA_QK_PAD), im),
                pl.BlockSpec((None, ts, GLA_V_PAD), im), pl.BlockSpec((None, ts, gw), im),
                pl.BlockSpec((gw, GLA_QK_PAD), const), pl.BlockSpec((1, GLA_QK_PAD), const),
                pl.BlockSpec((GCHUNK, GCHUNK), const)]

    scratch = [pltpu.VMEM((GLA_HEADS, GLA_DK_PAD, GLA_DV_PAD), F32), pltpu.VMEM((ts, GLA_V_PAD), F32)]
    o_f = pl.pallas_call(
        _gla_fwd_kernel,
        grid=(bsz, nblk),
        in_specs=seq_specs(False),
        out_specs=pl.BlockSpec((None, ts, GLA_V_PAD), fwd),
        out_shape=jax.ShapeDtypeStruct((bsz, s, GLA_V_PAD), BF16),
        scratch_shapes=scratch,
        compiler_params=_cparams("parallel", "arbitrary"),
        name="gla_fwd",
    )(q, k, v, g, w2f, bf, _gla_cum_table(False))
    return pl.pallas_call(
        _gla_bwd_kernel,
        grid=(bsz, nblk),
        in_specs=seq_specs(True) + [
            pl.BlockSpec((None, ts, GLA_V_PAD), bwd), pl.BlockSpec((None, ts, GLA_V_PAD), bwd),
            pl.BlockSpec((None, ts, FNET_WIDTH), bwd), pl.BlockSpec((None, ts, d), bwd),
            pl.BlockSpec((1, GLA_V_PAD), const), pl.BlockSpec((FNET_WIDTH, d), const),
            pl.BlockSpec((GLA_V_PAD, d), const)],
        out_specs=pl.BlockSpec((None, ts, d), bwd),
        out_shape=jax.ShapeDtypeStruct((bsz, s, d), F32),
        scratch_shapes=scratch,
        compiler_params=_cparams("parallel", "arbitrary"),
        name="gla_bwd_out",
    )(q, k, v, g, w2b, bb, _gla_cum_table(True), o_f, r, yf, h, gnorm, w_out_f, w_out_g)


MCHUNK = 128
assert MCHUNK == MLSTM_DH
MGATE_ROWS = 2 * MLSTM_HEADS
ZCOLS = 3 * MLSTM_DH


def _cummax_lanes(x, reverse):
    n = x.shape[-1]
    lane = lax.broadcasted_iota(jnp.int32, x.shape, 1)
    s = 1
    while s < n:
        if reverse:
            shifted = jnp.where(lane < n - s, pltpu.roll(x, n - s, axis=1), -jnp.inf)
        else:
            shifted = jnp.where(lane >= s, pltpu.roll(x, s, axis=1), -jnp.inf)
        x = jnp.maximum(x, shifted)
        s *= 2
    return x


def _mlstm_scan(q_ref, k_ref, v_ref, gi_ref, gf_ref, o_ref, st_ref, m_ref, reverse):
    ts = q_ref.shape[0]
    n = MCHUNK
    nck = ts // n
    dh = MLSTM_DH
    gr = MGATE_ROWS
    mask, tri_t_bf = _tri_masks(n, reverse)
    rbase = MLSTM_HEADS if reverse else 0

    @pl.when(pl.program_id(1) == 0)
    def _():
        st_ref[...] = jnp.zeros_like(st_ref)
        m_ref[...] = jnp.zeros_like(m_ref)

    gi = gi_ref[...].reshape(nck * gr, n) * LOG2E
    gf = gf_ref[...].reshape(nck * gr, n)
    cum = _cumsum_cols(_log2_sigmoid(gf), tri_t_bf)
    b = gi - cum
    md = cum + _cummax_lanes(b, reverse)
    tot = cum[:, 0:1] if reverse else cum[:, n - 1:n]
    w_end = tot + b
    m_loc = jnp.max(w_end, axis=1, keepdims=True)
    e = jnp.exp2(w_end - m_loc)
    ones = jnp.ones((n, dh), BF16)

    order = [nck - 1 - ci if reverse else ci for ci in range(nck)]
    heads = range(MLSTM_HEADS)
    grow = [slice(c * gr, (c + 1) * gr) for c in range(nck)]
    crow = [slice(c * n, (c + 1) * n) for c in range(nck)]
    hsl = [slice(h * dh, (h + 1) * dh) for h in heads]
    m_prev, fa, fb = [None] * nck, [None] * nck, [None] * nck
    m = m_ref[...]
    for c in order:
        m_prev[c] = m
        m_new = jnp.maximum(tot[grow[c]] + m, m_loc[grow[c]])
        fa[c] = jnp.exp2(tot[grow[c]] + m - m_new)
        fb[c] = jnp.exp2(m_loc[grow[c]] - m_new)
        m = m_new
    m_ref[...] = m
    z = [None] * nck
    for c in order:
        lw = cum[grow[c]] + m_prev[c][:, :n]
        m_t = jnp.maximum(lw, md[grow[c]])
        cols = [(cum[grow[c]] - m_t).T, (lw - m_t).T, (-m_t).T]
        z[c] = jnp.concatenate([jnp.broadcast_to(x[:, rbase + h:rbase + h + 1], (n, dh))
                                for h in heads for x in cols], axis=1)
    kt = [[k_ref[crow[c], hsl[h]].astype(F32).T for h in heads] for c in order]
    qk = [[_dot(q_ref[crow[c], hsl[h]], kt[i][h].astype(BF16)) for h in heads] for i, c in enumerate(order)]
    vaug = [[jnp.concatenate([v_ref[crow[c], hsl[h]], ones], axis=1) for h in heads] for c in order]
    s_qk, qa, floor, contrib = ([[None] * MLSTM_HEADS for _ in order] for _ in range(4))
    for i, c in enumerate(order):
        for h in heads:
            r = c * gr + rbase + h
            zb = h * ZCOLS
            expo = jnp.where(mask, z[c][:, zb:zb + dh] + b[r:r + 1, :], -jnp.inf)
            s_qk[i][h] = (qk[i][h] * jnp.exp2(expo)).astype(BF16)
            qa[i][h] = (q_ref[crow[c], hsl[h]].astype(F32) * jnp.exp2(z[c][:, zb + dh:zb + 2 * dh])).astype(BF16)
            floor[i][h] = jnp.exp2(z[c][:, zb + 2 * dh:zb + 3 * dh])
            ket = (kt[i][h] * e[r:r + 1, :]).astype(BF16)
            contrib[i][h] = _dot(ket, vaug[i][h])
    for i, c in enumerate(order):
        for h in heads:
            rr = rbase + h
            st = st_ref[h]
            res = _dot(jnp.concatenate([s_qk[i][h], qa[i][h]], axis=1),
                       jnp.concatenate([vaug[i][h], st.astype(BF16)], axis=0))
            o_ref[crow[c], hsl[h]] = res[:, :dh] / jnp.maximum(jnp.abs(res[:, dh:]), floor[i][h])
            st_ref[h] = fa[c][rr:rr + 1, :] * st + fb[c][rr:rr + 1, :] * contrib[i][h]


def _mlstm_fwd_kernel(q_ref, k_ref, v_ref, gi_ref, gf_ref, hf_ref, st_ref, m_ref, o_scr):
    _mlstm_scan(q_ref, k_ref, v_ref, gi_ref, gf_ref, o_scr, st_ref, m_ref, False)
    hf_ref[...] = o_scr[...].astype(BF16)


def _mlstm_bwd_kernel(q_ref, k_ref, v_ref, gi_ref, gf_ref, hf_ref, og_ref, sb_ref, t_ref, tp_ref, tn_ref,
                      h_ref, mn_ref, cw_ref, cb_ref, wc_ref, wm_ref, out_ref, st_ref, m_ref, o_scr):
    _mlstm_scan(q_ref, k_ref, v_ref, gi_ref, gf_ref, o_scr, st_ref, m_ref, True)
    ts = q_ref.shape[0]
    i = pl.program_id(1)
    nblk = pl.num_programs(1)
    prev = jnp.where(i < nblk - 1, tp_ref[...].astype(F32), 0.0)
    nxt = jnp.where(i > 0, tn_ref[...].astype(F32), 0.0)
    ext = jnp.concatenate([prev, t_ref[...].astype(F32), nxt], axis=0)
    rows = ts + 2 * HALO_BF16
    lo = pltpu.roll(ext, 1, axis=0)[HALO_BF16:HALO_BF16 + ts]
    hi = pltpu.roll(ext, rows - 1, axis=0)[HALO_BF16:HALO_BF16 + ts]
    cw = cw_ref[...]
    conv = lo * cw[0:1, :] + ext[HALO_BF16:HALO_BF16 + ts] * cw[1:2, :] + hi * cw[2:3, :] + cb_ref[...]
    yc = (sb_ref[...].astype(F32) * conv).astype(BF16)
    for r0 in range(0, ts, min(EPILOGUE_SUB, ts)):
        rs = slice(r0, r0 + min(EPILOGUE_SUB, ts))
        o = o_scr[rs, :] + hf_ref[rs, :].astype(F32)
        normed = []
        for h in range(MLSTM_HEADS):
            oh = o[:, h * MLSTM_DH:(h + 1) * MLSTM_DH]
            normed.append(oh * lax.rsqrt(jnp.mean(oh * oh, axis=-1, keepdims=True) + EPS))
        ym = jnp.concatenate(normed, axis=1) * mn_ref[...]
        ym = (ym * jax.nn.sigmoid(og_ref[rs, :].astype(F32))).astype(BF16)
        out_ref[rs, :] = h_ref[rs, :] + _dot(yc[rs, :], wc_ref[...]) + _dot(ym, wm_ref[...])


def _mlstm(q, k, v, gi, gf, og, sb, t, h, mnorm, conv_w, conv_b, w_out_c, w_out_m, ts):
    bsz, s, _ = q.shape
    d = h.shape[-1]
    nblk = s // ts
    nck = ts // MCHUNK
    hb = ts // HALO_BF16
    fwd = lambda b, i: (b, i, 0)
    bwd = lambda b, i: (b, nblk - 1 - i, 0)
    const = lambda b, i: (0, 0)

    def seq_specs(rev):
        w = MLSTM_WIDTH
        blk = (lambda i: nblk - 1 - i) if rev else (lambda i: i)
        im = lambda b, i: (b, blk(i), 0)
        im4 = lambda b, i: (b, blk(i), 0, 0)
        gspec = pl.BlockSpec((None, nck, MGATE_ROWS, MCHUNK), im4)
        return [pl.BlockSpec((None, ts, w), im), pl.BlockSpec((None, ts, w), im), pl.BlockSpec((None, ts, w), im),
                gspec, gspec]

    scratch = [pltpu.VMEM((MLSTM_HEADS, MLSTM_DH, 2 * MLSTM_DH), F32), pltpu.VMEM((MGATE_ROWS, 2 * MLSTM_DH), F32),
               pltpu.VMEM((ts, MLSTM_WIDTH), F32)]
    h_f = pl.pallas_call(
        _mlstm_fwd_kernel,
        grid=(bsz, nblk),
        in_specs=seq_specs(False),
        out_specs=pl.BlockSpec((None, ts, MLSTM_WIDTH), fwd),
        out_shape=jax.ShapeDtypeStruct((bsz, s, MLSTM_WIDTH), BF16),
        scratch_shapes=scratch,
        compiler_params=_cparams("parallel", "arbitrary"),
        name="mlstm_fwd",
    )(q, k, v, gi, gf)
    last16 = s // HALO_BF16 - 1
    halo_prev = lambda b, i: (b, jnp.maximum((nblk - 1 - i) * hb - 1, 0), 0)
    halo_next = lambda b, i: (b, jnp.minimum((nblk - i) * hb, last16), 0)
    return pl.pallas_call(
        _mlstm_bwd_kernel,
        grid=(bsz, nblk),
        in_specs=seq_specs(True) + [
            pl.BlockSpec((None, ts, MLSTM_WIDTH), bwd), pl.BlockSpec((None, ts, MLSTM_WIDTH), bwd),
            pl.BlockSpec((None, ts, CONV_WIDTH), bwd), pl.BlockSpec((None, ts, CONV_WIDTH), bwd),
            pl.BlockSpec((None, HALO_BF16, CONV_WIDTH), halo_prev),
            pl.BlockSpec((None, HALO_BF16, CONV_WIDTH), halo_next),
            pl.BlockSpec((None, ts, d), bwd),
            pl.BlockSpec((1, MLSTM_WIDTH), const), pl.BlockSpec((3, CONV_WIDTH), const),
            pl.BlockSpec((1, CONV_WIDTH), const), pl.BlockSpec((CONV_WIDTH, d), const),
            pl.BlockSpec((MLSTM_WIDTH, d), const)],
        out_specs=pl.BlockSpec((None, ts, d), bwd),
        out_shape=jax.ShapeDtypeStruct((bsz, s, d), F32),
        scratch_shapes=scratch,
        compiler_params=_cparams("parallel", "arbitrary"),
        name="mlstm_bwd_out",
    )(q, k, v, gi, gf, h_f, og, sb, t, t, t, h, mnorm, conv_w, conv_b, w_out_c, w_out_m)


FFN_SUB = ROW_TILE


def _ffn_kernel(h_ref, hp_ref, hn_ref, p_ref, fg_ref, wu_ref, cw_ref, cb_ref, wd_ref, pg_ref, pgw_ref, pw_ref,
                fn_ref, out_ref, xn_scr, act_scr, *, d_ff, f_tile, final):
    ts = h_ref.shape[0]
    i = pl.program_id(1)
    nblk = pl.num_programs(1)
    fg = fg_ref[...]
    sub = min(FFN_SUB, ts)
    rows = sub + 2 * HALO
    for k, r0 in enumerate(range(0, ts, sub)):
        x = h_ref[r0:r0 + sub, :]
        if r0 == 0:
            prev = jnp.where(i > 0, _rms(hp_ref[...], fg), 0.0)
        else:
            prev = _rms(h_ref[r0 - HALO:r0, :], fg)
        if r0 + sub == ts:
            nxt = jnp.where(i < nblk - 1, _rms(hn_ref[...], fg), 0.0)
        else:
            nxt = _rms(h_ref[r0 + sub:r0 + sub + HALO, :], fg)
        xn_scr[k, 0:HALO, :] = prev.astype(BF16)
        xn_scr[k, HALO:HALO + sub, :] = _rms(x, fg).astype(BF16)
        xn_scr[k, HALO + sub:, :] = nxt.astype(BF16)
        xe = xn_scr[k]
        xc = xe[HALO:HALO + sub]
        for f0 in range(0, d_ff, f_tile):
            gate = _dot(xe, wu_ref[:, f0:f0 + f_tile])
            val = _dot(xc, wu_ref[:, d_ff + f0:d_ff + f0 + f_tile])
            cw = cw_ref[:, f0:f0 + f_tile]
            lo = pltpu.roll(gate, 1, axis=0)[HALO:HALO + sub]
            hi = pltpu.roll(gate, rows - 1, axis=0)[HALO:HALO + sub]
            conv = (lo * cw[0:1, :] + gate[HALO:HALO + sub] * cw[1:2, :] + hi * cw[2:3, :]
                    + cb_ref[:, f0:f0 + f_tile])
            act_scr[r0:r0 + sub, f0:f0 + f_tile] = (jax.nn.silu(conv) * val).astype(BF16)
        h2 = x + _dot(act_scr[r0:r0 + sub, :], wd_ref[...])
        gate = jax.nn.sigmoid(_dot(_rms(h2, pg_ref[...]).astype(BF16), pgw_ref[...]))
        h3 = h2 + _dot(p_ref[r0:r0 + sub, :].astype(BF16), pw_ref[...]) * gate
        out_ref[r0:r0 + sub, :] = _rms(h3, fn_ref[...]) if final else h3


def _ffn(h, p, layer, fgain, w_up, conv_w, conv_b, w_down, pgain, pg_w, p_w, final_gain, final, ts, f_tile):
    bsz, s, d = h.shape
    d_ff = w_down.shape[0]
    nblk = s // ts
    hb = ts // HALO
    last8 = s // HALO - 1
    blk = lambda b, i: (b, i, 0)
    const = lambda b, i: (0, 0)
    once = pl.Buffered(1)

    def wspec(shape):
        return pl.BlockSpec(shape, const, pipeline_mode=once)

    return pl.pallas_call(
        functools.partial(_ffn_kernel, d_ff=d_ff, f_tile=f_tile, final=final),
        grid=(bsz, nblk),
        in_specs=[pl.BlockSpec((None, ts, d), blk),
                  pl.BlockSpec((None, HALO, d), lambda b, i: (b, jnp.maximum(i * hb - 1, 0), 0)),
                  pl.BlockSpec((None, HALO, d), lambda b, i: (b, jnp.minimum((i + 1) * hb, last8), 0)),
                  pl.BlockSpec((None, None, ts, p.shape[-1]), lambda b, i: (layer, b, i, 0)),
                  wspec((1, d)), wspec((d, 2 * d_ff)), wspec((3, d_ff)), wspec((1, d_ff)), wspec((d_ff, d)),
                  wspec((1, d)), wspec((d, d)), wspec((p.shape[-1], d)), wspec((1, d))],
        out_specs=pl.BlockSpec((None, ts, d), blk),
        out_shape=jax.ShapeDtypeStruct((bsz, s, d), F32),
        scratch_shapes=[pltpu.VMEM((-(-ts // FFN_SUB), min(FFN_SUB, ts) + 2 * HALO, d), BF16),
                        pltpu.VMEM((ts, d_ff), BF16)],
        compiler_params=_cparams("parallel", "parallel"),
        name="ffn_ple",
    )(h, h, h, p, fgain, w_up, conv_w, conv_b, w_down, pgain, pg_w, p_w, final_gain)


def _pad_heads_cols(w, heads, dh, dpad):
    lead = w.shape[:-1]
    w = w.reshape(lead + (heads, dh))
    w = jnp.pad(w, [(0, 0)] * len(lead) + [(0, 0), (0, dpad - dh)])
    return w.reshape(lead + (heads * dpad,))


def _pad_cols(w, width):
    return jnp.pad(w, [(0, 0)] * (w.ndim - 1) + [(0, width - w.shape[-1])])


def _even_weights(w_in, w2_f, b_f, w2_b, b_b, gnorm, w_out):
    cuts = np.cumsum([FNET_WIDTH, GLA_HEADS * GLA_DK, GLA_HEADS * GLA_DK, GLA_HEADS * GLA_DV,
                      GLA_HEADS * GLA_DV]).tolist()
    wu, wq, wk, wv, wr, wg = jnp.split(w_in, cuts, axis=-1)
    w_pad = jnp.concatenate([
        wu, _pad_heads_cols(wq * (GLA_DK ** -0.5), GLA_HEADS, GLA_DK, GLA_DK_PAD),
        _pad_heads_cols(wk, GLA_HEADS, GLA_DK, GLA_DK_PAD),
        _pad_heads_cols(wv, GLA_HEADS, GLA_DV, GLA_DV_PAD), _pad_heads_cols(wr, GLA_HEADS, GLA_DV, GLA_DV_PAD),
        _pad_cols(wg, GATE_PAD)], axis=-1).astype(BF16)
    zero = jnp.zeros((GLA_GATE_RANK, GLA_QK_PAD), F32)
    w2f = jnp.concatenate([_pad_heads_cols(w2_f, GLA_HEADS, GLA_DK, GLA_DK_PAD), zero], axis=0)
    w2b = jnp.concatenate([zero, _pad_heads_cols(w2_b, GLA_HEADS, GLA_DK, GLA_DK_PAD)], axis=0)
    bf = _pad_heads_cols(b_f[None, :], GLA_HEADS, GLA_DK, GLA_DK_PAD)
    bb = _pad_heads_cols(b_b[None, :], GLA_HEADS, GLA_DK, GLA_DK_PAD)
    gn = _pad_heads_cols(gnorm[None, :], GLA_HEADS, GLA_DV, GLA_DV_PAD)
    w_out_f = w_out[:FNET_WIDTH].astype(BF16)
    wog = w_out[FNET_WIDTH:].reshape(GLA_HEADS, GLA_DV, -1)
    wog = jnp.pad(wog, ((0, 0), (0, GLA_DV_PAD - GLA_DV), (0, 0))).reshape(GLA_V_PAD, -1).astype(BF16)
    return w_pad, w2f, bf, w2b, bb, gn, w_out_f, wog


def _trunk(x, p, e_norm, e_w_in, e_gla_w2_f, e_gla_b_f, e_gla_w2_b, e_gla_b_b, e_gla_norm, e_w_out,
           o_norm, o_w_in, o_conv_w, o_conv_b, o_gate_bias, o_mlstm_norm, o_w_out,
           ffn_norm, ffn_w_up, ffn_conv_w, ffn_conv_b, ffn_w_down,
           ple_w, ple_gate_norm, ple_gate_w, final_norm, *, ts=ROW_TILE, f_tile=FFN_TILE):
    bsz, s, d = x.shape
    depth = p.shape[0]
    t = bsz * s
    ts = min(ts, s)
    h = x
    for layer in range(depth):
        j = layer // 2
        h2d = h.reshape(t, d)
        if layer % 2 == 0:
            w_pad, w2f, bf, w2b, bb, gn, w_out_f, w_out_g = _even_weights(
                e_w_in[j], e_gla_w2_f[j], e_gla_b_f[j], e_gla_w2_b[j], e_gla_b_b[j], e_gla_norm[j], e_w_out[j])
            a, b, q, k, v, r, g = _inproj_even(h2d, e_norm[j][None, :], w_pad, _channel_dft(), ts)
            sh = lambda z: z.reshape(bsz, s, z.shape[-1])
            yf = _fnet(sh(a), sh(b))
            h = _gla(sh(q), sh(k), sh(v), sh(g), w2f, bf, w2b, bb, sh(r), yf, h, gn, w_out_f, w_out_g, ts)
        else:
            q_lo = 3 * CONV_WIDTH
            col = jnp.arange(o_w_in.shape[-1])
            qscale = jnp.where((col >= q_lo) & (col < q_lo + MLSTM_WIDTH), MLSTM_DH ** -0.5, 1.0)
            w_pad = _pad_cols(o_w_in[j] * qscale, ODD_W).astype(BF16)
            sb, tt, q, k, v, og, gates = _inproj_odd(h2d, o_norm[j][None, :], w_pad, o_gate_bias[j][None, :], ts)
            sh = lambda z: z.reshape(bsz, s, z.shape[-1])
            nh = MLSTM_HEADS
            gates = gates.reshape(bsz, s // MCHUNK, MCHUNK, ODD_GATES).transpose(0, 1, 3, 2)
            gi = jnp.concatenate([gates[:, :, 0:nh], gates[:, :, 2 * nh:3 * nh]], axis=2)
            gf = jnp.concatenate([gates[:, :, nh:2 * nh], gates[:, :, 3 * nh:4 * nh]], axis=2)
            w_out = o_w_out[j].astype(BF16)
            h = _mlstm(sh(q), sh(k), sh(v), gi, gf, sh(og), sh(sb), sh(tt), h,
                       o_mlstm_norm[j][None, :], o_conv_w[j], o_conv_b[j][None, :],
                       w_out[:CONV_WIDTH], w_out[CONV_WIDTH:], ts)
        h = _ffn(h, p, layer, ffn_norm[layer][None, :], ffn_w_up[layer].astype(BF16), ffn_conv_w[layer],
                 ffn_conv_b[layer][None, :], ffn_w_down[layer].astype(BF16), ple_gate_norm[layer][None, :],
                 ple_gate_w[layer].astype(BF16), ple_w[layer].astype(BF16), final_norm[None, :],
                 layer == depth - 1, ts, f_tile)
    return h


def kernel(x_prompt, x_sample, p_prompt, p_sample, e_norm, e_w_in, e_gla_w2_f, e_gla_b_f, e_gla_w2_b, e_gla_b_b, e_gla_norm, e_w_out, o_norm, o_w_in, o_conv_w, o_conv_b, o_gate_bias, o_mlstm_norm, o_w_out, ffn_norm, ffn_w_up, ffn_conv_w, ffn_conv_b, ffn_w_down, ple_w, ple_gate_norm, ple_gate_w, final_norm):
    weights = (e_norm, e_w_in, e_gla_w2_f, e_gla_b_f, e_gla_w2_b, e_gla_b_b, e_gla_norm, e_w_out,
               o_norm, o_w_in, o_conv_w, o_conv_b, o_gate_bias, o_mlstm_norm, o_w_out,
               ffn_norm, ffn_w_up, ffn_conv_w, ffn_conv_b, ffn_w_down,
               ple_w, ple_gate_norm, ple_gate_w, final_norm)
    return (_trunk(x_prompt, p_prompt, *weights), _trunk(x_sample, p_sample, *weights))
```
